```python
import jax, jax.numpy as jnp
from jax import lax
import numpy as np

D_MODEL = 1024
BATCH = 8
SEQ = 2048
DEPTH = 2
DEC_BATCH = 32
DEC_SEQ = 1
PAST_LEN = 16384
PAGE_SIZE = 128

MIX_W = D_MODEL // 2
HEAD_DIM = 64
N_HEADS = MIX_W // HEAD_DIM
ATT_W = N_HEADS * HEAD_DIM
Q_BLOCK = 128
SB_BIAS_INIT = -6.0
LRU_W = MIX_W
LRU_BLOCKS = 8
LRU_BLOCK_W = LRU_W // LRU_BLOCKS
LRU_C = 8.0
CONV_W = 4
POOL_W = MIX_W
POOL_WINDOWS = (2, 4, 8, 16)
N_POOL_GROUPS = len(POOL_WINDOWS)
POOL_GROUP = POOL_W // N_POOL_GROUPS
POOL_HIST = max(POOL_WINDOWS) - 1
N_BRANCH = 3
D_FF = 4 * D_MODEL
RMS_EPS = 1e-6
IN_COLS = 3 * ATT_W + LRU_W + POOL_W + N_BRANCH * D_MODEL

kernel_name = "hybrid_stickbreak_rglru_pool_decoder_step"


def rms_norm(x, g):
    xf = x.astype(jnp.float32)
    y = xf * lax.rsqrt(jnp.mean(xf * xf, axis=-1, keepdims=True) + RMS_EPS)
    return (y * g.astype(jnp.float32)).astype(x.dtype)


def stick_breaking_attention(q, k, v, q_pos, k_pos, sb_bias):
    bsz, tq, nh, dh = q.shape
    qb = min(Q_BLOCK, tq)
    n_blk = -(-tq // qb)
    pad = n_blk * qb - tq
    q = jnp.pad(q, ((0, 0), (0, pad), (0, 0), (0, 0)))
    q_pos = jnp.pad(q_pos, (0, pad))
    q_blocks = jnp.moveaxis(q.reshape(bsz, n_blk, qb, nh, dh), 1, 0)
    p_blocks = q_pos.reshape(n_blk, qb)
    scale = dh ** -0.5
    bias = sb_bias.astype(jnp.float32)[None, :, None, None]

    def one_block(args):
        q_blk, p_blk = args
        z = jnp.einsum("bqhd,bkhd->bhqk", q_blk, k, preferred_element_type=jnp.float32) * scale + bias
        mask = (k_pos[None, :] < p_blk[:, None])[None, None]
        log_beta = jax.nn.log_sigmoid(z)
        log_keep = jnp.where(mask, jax.nn.log_sigmoid(-z), 0.0)
        log_after = lax.cumsum(log_keep, axis=3, reverse=True) - log_keep
        w = jnp.where(mask, jnp.exp(log_beta + log_after), 0.0)
        return jnp.einsum("bhqk,bkhd->bqhd", w.astype(v.dtype), v, preferred_element_type=jnp.float32)

    out = lax.map(one_block, (q_blocks, p_blocks))
    out = jnp.moveaxis(out, 0, 1).reshape(bsz, n_blk * qb, nh, dh)[:, :tq]
    return out.astype(q.dtype)


def causal_depthwise_conv(x_hist, w, bias):
    t = x_hist.shape[1] - (CONV_W - 1)
    y = bias
    for j in range(CONV_W):
        y = y + x_hist[:, j:j + t] * w[j]
    return y


def rg_lru(x, pos, h0, wa, ba, wx, bx, lam):
    bsz, t, c = x.shape
    xb = x.reshape(bsz, t, LRU_BLOCKS, LRU_BLOCK_W)
    r = jax.nn.sigmoid((jnp.einsum("btnc,ncd->btnd", xb, wa).reshape(bsz, t, c) + ba).astype(jnp.float32))
    i = jax.nn.sigmoid((jnp.einsum("btnc,ncd->btnd", xb, wx).reshape(bsz, t, c) + bx).astype(jnp.float32))
    log_a = -LRU_C * r * jax.nn.softplus(-lam.astype(jnp.float32))
    a = jnp.exp(log_a)
    mult = jnp.sqrt(-jnp.expm1(2.0 * log_a))
    reset = (pos == 0)[None, :, None]
    a = jnp.where(reset, 0.0, a)
    mult = jnp.where(reset, 1.0, mult)
    u = mult * i * x.astype(jnp.float32)
    u = u.at[:, 0].add(a[:, 0] * h0.astype(jnp.float32))

    def combine(left, right):
        a_l, u_l = left
        a_r, u_r = right
        return a_l * a_r, a_r * u_l + u_r

    _, h = lax.associative_scan(combine, (a, u), axis=1)
    return h.astype(x.dtype), h[:, -1].astype(x.dtype)


def multi_scale_pool(xh, pos, pool_w, pool_scale):
    bsz, length, width = xh.shape
    t = pos.shape[0]
    n_hist = length - t
    xf = xh.astype(jnp.float32)
    csum = jnp.concatenate([jnp.zeros((bsz, 1, width), jnp.float32), lax.cumsum(xf, axis=1)], axis=1)
    incl = csum[:, n_hist + 1:]
    x_t = xf[:, n_hist:]
    diffs = []
    for g, win in enumerate(POOL_WINDOWS):
        c0, c1 = g * POOL_GROUP, (g + 1) * POOL_GROUP
        start = csum[:, n_hist + 1 - win:n_hist + 1 - win + t, c0:c1]
        cnt = jnp.minimum(pos + 1, win).astype(jnp.float32)[None, :, None]
        diffs.append((incl[..., c0:c1] - start) / cnt - x_t[..., c0:c1])
    d = jnp.stack(diffs, axis=2).astype(xh.dtype)
    y = jnp.einsum("btgc,gcd->btgd", d, pool_w).reshape(bsz, t, width)
    return y * pool_scale


def trunk_layer(x, pos, k_past, v_past, conv_hist, pool_hist, h0, lp):
    bsz, t, _ = x.shape
    u = rms_norm(x, lp["norm_mix_pre"])
    proj = jnp.einsum("btd,de->bte", u, lp["w_in"])
    cuts = [ATT_W, 2 * ATT_W, 3 * ATT_W, 3 * ATT_W + LRU_W, 3 * ATT_W + LRU_W + POOL_W]
    q, k, v, xr, xp, g = jnp.split(proj, cuts, axis=-1)
    q = q.reshape(bsz, t, N_HEADS, HEAD_DIM)
    k = k.reshape(bsz, t, N_HEADS, HEAD_DIM)
    v = v.reshape(bsz, t, N_HEADS, HEAD_DIM)
    if k_past is None:
        k_all, v_all, k_pos = k, v, pos
    else:
        k_all = jnp.concatenate([k_past, k], axis=1)
        v_all = jnp.concatenate([v_past, v], axis=1)
        k_pos = jnp.concatenate([jnp.arange(k_past.shape[1], dtype=jnp.int32), pos])
    o_a = stick_breaking_attention(q, k_all, v_all, pos, k_pos, lp["sb_bias"]).reshape(bsz, t, ATT_W)
    xr_h = jnp.concatenate([conv_hist, xr], axis=1)
    xc = causal_depthwise_conv(xr_h, lp["conv_w"], lp["conv_b"])
    o_b, h_last = rg_lru(xc, pos, h0, lp["gate_a_w"], lp["gate_a_b"], lp["gate_x_w"], lp["gate_x_b"], lp["lru_lambda"])
    conv_new = xr_h[:, -(CONV_W - 1):]
    xp_h = jnp.concatenate([pool_hist, xp], axis=1)
    o_c = multi_scale_pool(xp_h, pos, lp["pool_w"], lp["pool_scale"])
    pool_new = xp_h[:, -POOL_HIST:]
    branches = jnp.stack([o_a, o_b, o_c], axis=2)
    gates = jax.nn.sigmoid(g.reshape(bsz, t, N_BRANCH, D_MODEL).astype(jnp.float32)).astype(x.dtype)
    per_branch = jnp.einsum("btnc,ncd->btnd", branches, lp["w_branch"])
    mixed = jnp.sum(gates * per_branch, axis=2)
    mix_out = jnp.einsum("btd,de->bte", mixed, lp["w_out"])
    x = x + rms_norm(mix_out, lp["norm_mix_post"])
    u2 = rms_norm(x, lp["norm_mlp_pre"])
    hid = jnp.square(jax.nn.relu(jnp.einsum("btd,df->btf", u2, lp["w_up"])))
    ffn = jnp.einsum("btf,fd->btd", hid, lp["w_down"])
    x = x + rms_norm(ffn, lp["norm_mlp_post"])
    return x, k, v, h_last, conv_new, pool_new


def setup_inputs(seed: int = 0) -> dict:
    key = jax.random.key(seed)
    ks = jax.random.split(key, 32)
    f32 = jnp.float32
    n_pages = PAST_LEN // PAGE_SIZE
    n_used = DEC_BATCH * n_pages
    n_phys = n_used + max(1, n_used // 4)

    def nrm(k, shape, scale):
        return jax.random.normal(k, shape, f32) * scale

    x_prompt = nrm(ks[0], (BATCH, SEQ, D_MODEL), 1.0)
    x_sample = nrm(ks[1], (DEC_BATCH, DEC_SEQ, D_MODEL), 1.0)
    cache_k = nrm(ks[2], (DEPTH, n_phys, PAGE_SIZE, N_HEADS, HEAD_DIM), 1.0)
    cache_v = nrm(ks[3], (DEPTH, n_phys, PAGE_SIZE, N_HEADS, HEAD_DIM), 1.0)
    page_table = jax.random.permutation(ks[4], n_phys)[:n_used].reshape(DEC_BATCH, n_pages).astype(jnp.int32)
    state_h = nrm(ks[5], (DEPTH, DEC_BATCH, LRU_W), 0.5)
    state_conv = nrm(ks[6], (DEPTH, DEC_BATCH, CONV_W - 1, LRU_W), 1.0)
    state_pool = nrm(ks[7], (DEPTH, DEC_BATCH, POOL_HIST, POOL_W), 1.0)
    norm_mix_pre = 1.0 + nrm(ks[8], (DEPTH, D_MODEL), 0.05)
    norm_mix_post = 1.0 + nrm(ks[9], (DEPTH, D_MODEL), 0.05)
    norm_mlp_pre = 1.0 + nrm(ks[10], (DEPTH, D_MODEL), 0.05)
    norm_mlp_post = 1.0 + nrm(ks[11], (DEPTH, D_MODEL), 0.05)
    w_in = nrm(ks[12], (DEPTH, D_MODEL, IN_COLS), D_MODEL ** -0.5)
    sb_bias = SB_BIAS_INIT + nrm(ks[26], (DEPTH, N_HEADS), 0.3)
    conv_w = nrm(ks[13], (DEPTH, CONV_W, LRU_W), CONV_W ** -0.5)
    conv_b = nrm(ks[14], (DEPTH, LRU_W), 0.01)
    gate_a_w = nrm(ks[15], (DEPTH, LRU_BLOCKS, LRU_BLOCK_W, LRU_BLOCK_W), LRU_BLOCK_W ** -0.5)
    gate_a_b = nrm(ks[16], (DEPTH, LRU_W), 0.01)
    gate_x_w = nrm(ks[17], (DEPTH, LRU_BLOCKS, LRU_BLOCK_W, LRU_BLOCK_W), LRU_BLOCK_W ** -0.5)
    gate_x_b = nrm(ks[18], (DEPTH, LRU_W), 0.01)
    a_c = jax.random.uniform(ks[19], (DEPTH, LRU_W), f32, 0.9, 0.999)
    a_base = a_c ** (1.0 / LRU_C)
    lru_lambda = jnp.log(a_base) - jnp.log1p(-a_base)
    pool_w = nrm(ks[20], (DEPTH, N_POOL_GROUPS, POOL_GROUP, POOL_GROUP), POOL_GROUP ** -0.5)
    pool_scale = 1.0 + nrm(ks[21], (DEPTH, POOL_W), 0.05)
    w_branch = nrm(ks[22], (DEPTH, N_BRANCH, MIX_W, D_MODEL), MIX_W ** -0.5)
    w_out = nrm(ks[23], (DEPTH, D_MODEL, D_MODEL), D_MODEL ** -0.5)
    w_up = nrm(ks[24], (DEPTH, D_MODEL, D_FF), D_MODEL ** -0.5)
    w_down = nrm(ks[25], (DEPTH, D_FF, D_MODEL), D_FF ** -0.5)
    return {
        "x_prompt": x_prompt, "x_sample": x_sample,
        "cache_k": cache_k, "cache_v": cache_v, "page_table": page_table,
        "state_h": state_h, "state_conv": state_conv, "state_pool": state_pool,
        "norm_mix_pre": norm_mix_pre, "norm_mix_post": norm_mix_post,
        "norm_mlp_pre": norm_mlp_pre, "norm_mlp_post": norm_mlp_post,
        "w_in": w_in, "sb_bias": sb_bias, "conv_w": conv_w, "conv_b": conv_b,
        "gate_a_w": gate_a_w, "gate_a_b": gate_a_b, "gate_x_w": gate_x_w, "gate_x_b": gate_x_b,
        "lru_lambda": lru_lambda, "pool_w": pool_w, "pool_scale": pool_scale,
        "w_branch": w_branch, "w_out": w_out, "w_up": w_up, "w_down": w_down,
    }


def reference(x_prompt, x_sample, cache_k, cache_v, page_table, state_h, state_conv, state_pool,
              norm_mix_pre, norm_mix_post, norm_mlp_pre, norm_mlp_post,
              w_in, sb_bias, conv_w, conv_b, gate_a_w, gate_a_b, gate_x_w, gate_x_b,
              lru_lambda, pool_w, pool_scale, w_branch, w_out, w_up, w_down):
    b_p, seq = x_prompt.shape[0], x_prompt.shape[1]
    b_s, dec_seq = x_sample.shape[0], x_sample.shape[1]
    past_len = page_table.shape[1] * cache_k.shape[2]
    pos_p = jnp.arange(seq, dtype=jnp.int32)
    pos_s = past_len + jnp.arange(dec_seq, dtype=jnp.int32)
    zero_conv = jnp.zeros((b_p, CONV_W - 1, LRU_W), x_prompt.dtype)
    zero_pool = jnp.zeros((b_p, POOL_HIST, POOL_W), x_prompt.dtype)
    zero_h = jnp.zeros((b_p, LRU_W), x_prompt.dtype)

    yp, ys = x_prompt, x_sample
    kp, vp, ksm, vsm, hp, hs, cp, csm, pp, psm = [], [], [], [], [], [], [], [], [], []
    for l in range(DEPTH):
        lp = {
            "norm_mix_pre": norm_mix_pre[l], "norm_mix_post": norm_mix_post[l],
            "norm_mlp_pre": norm_mlp_pre[l], "norm_mlp_post": norm_mlp_post[l],
            "w_in": w_in[l], "sb_bias": sb_bias[l], "conv_w": conv_w[l], "conv_b": conv_b[l],
            "gate_a_w": gate_a_w[l], "gate_a_b": gate_a_b[l],
            "gate_x_w": gate_x_w[l], "gate_x_b": gate_x_b[l],
            "lru_lambda": lru_lambda[l], "pool_w": pool_w[l], "pool_scale": pool_scale[l],
            "w_branch": w_branch[l], "w_out": w_out[l], "w_up": w_up[l], "w_down": w_down[l],
        }
        yp, k_new, v_new, h_new, c_new, p_new = trunk_layer(yp, pos_p, None, None, zero_conv, zero_pool, zero_h, lp)
        kp.append(k_new); vp.append(v_new); hp.append(h_new); cp.append(c_new); pp.append(p_new)
        k_past = cache_k[l][page_table].reshape(b_s, past_len, N_HEADS, HEAD_DIM)
        v_past = cache_v[l][page_table].reshape(b_s, past_len, N_HEADS, HEAD_DIM)
        ys, k_new, v_new, h_new, c_new, p_new = trunk_layer(ys, pos_s, k_past, v_past, state_conv[l], state_pool[l], state_h[l], lp)
        ksm.append(k_new); vsm.append(v_new); hs.append(h_new); csm.append(c_new); psm.append(p_new)

    y_prompt = yp
    y_sample = ys
    k_prompt = jnp.stack(kp)
    v_prompt = jnp.stack(vp)
    k_sample = jnp.stack(ksm)
    v_sample = jnp.stack(vsm)
    h_prompt = jnp.stack(hp)
    h_sample = jnp.stack(hs)
    conv_prompt = jnp.stack(cp)
    conv_sample = jnp.stack(csm)
    pool_prompt = jnp.stack(pp)
    pool_sample = jnp.stack(psm)
    return (y_prompt, y_sample, k_prompt, v_prompt, k_sample, v_sample, h_prompt, h_sample, conv_prompt, conv_sample, pool_prompt, pool_sample)
```

```python
import functools

import jax
import jax.numpy as jnp
from jax import lax
from jax.experimental import pallas as pl
from jax.experimental.pallas import tpu as pltpu

F32 = jnp.float32
BF16 = jnp.bfloat16

HEAD_DIM = 64
LRU_C = 8.0
CONV_W = 4
POOL_WINDOWS = (2, 4, 8, 16)
RMS_EPS = 1e-6

LANES = 128
SUBLANES = 8
HEADS_PER_BLOCK = LANES // HEAD_DIM
VMEM_LIMIT_BYTES = 56 * 1024 * 1024

ROW_TILE = 256
ATT_TILE = 256
SEQ_CHUNK = 256
PAGES_PER_STEP = 8

_NT = (((1,), (1,)), ((), ()))


def _params(*sem):
    return pltpu.CompilerParams(dimension_semantics=sem, vmem_limit_bytes=VMEM_LIMIT_BYTES)


def _resident(shape):
    zeros = (0,) * len(shape)
    return pl.BlockSpec(shape, lambda *_: zeros, pipeline_mode=pl.Buffered(1))


def _rms(x, g):
    return x * lax.rsqrt(jnp.mean(x * x, axis=-1, keepdims=True) + RMS_EPS) * g


def _softplus(x):
    return jnp.maximum(x, 0.0) + jnp.log1p(jnp.exp(-jnp.abs(x)))


def _stick_logs(z):
    l = jnp.log1p(jnp.exp(-jnp.abs(z)))
    return jnp.minimum(z, 0.0) - l, jnp.minimum(-z, 0.0) - l


def _split_bf16(x):
    hi = x.astype(BF16)
    lo = (x - hi.astype(F32)).astype(BF16)
    return hi, lo


def _prompt_in_proj_kernel(x_ref, g_ref, wq_ref, wkv_ref, wrp_ref, qb_ref, kv_ref, kvb_ref, xr_ref, xp_ref):
    w = qb_ref.shape[-1]
    u = _rms(x_ref[0], g_ref[...]).astype(BF16)
    qb_ref[0] = (jnp.dot(u, wq_ref[...], preferred_element_type=F32) * (HEAD_DIM ** -0.5)).astype(BF16)
    kv = lax.dot_general(wkv_ref[...], u, _NT, preferred_element_type=F32)
    kv_ref[:, 0] = kv.reshape(2, w, kv.shape[1])
    kvb_ref[:, 0, 0] = kv.reshape(2, w, kv.shape[1]).astype(BF16)
    rp = jnp.dot(u, wrp_ref[...], preferred_element_type=F32)
    xr_ref[0] = rp[:, :w]
    xp_ref[0] = rp[:, w:]


def _prompt_in_proj(x, g, wq, wkv_t, wrp, tm):
    bsz, t, d = x.shape
    width = wq.shape[1]
    rows = pl.BlockSpec((1, tm, width), lambda b, i: (b, i, 0))
    return pl.pallas_call(
        _prompt_in_proj_kernel,
        grid=(bsz, t // tm),
        in_specs=[pl.BlockSpec((1, tm, d), lambda b, i: (b, i, 0)), _resident((1, d)),
                  _resident(wq.shape), _resident(wkv_t.shape), _resident(wrp.shape)],
        out_specs=[rows,
                   pl.BlockSpec((2, 1, width, tm), lambda b, i: (0, b, 0, i)),
                   pl.BlockSpec((2, 1, 1, width, tm), lambda b, i: (0, b, i, 0, 0)),
                   rows, rows],
        out_shape=[jax.ShapeDtypeStruct((bsz, t, width), BF16),
                   jax.ShapeDtypeStruct((2, bsz, width, t), F32),
                   jax.ShapeDtypeStruct((2, bsz, t // tm, width, tm), BF16),
                   jax.ShapeDtypeStruct((bsz, t, width), F32),
                   jax.ShapeDtypeStruct((bsz, t, width), F32)],
        compiler_params=_params("parallel", "parallel"),
        name="prompt_in_proj",
    )(x, g, wq, wkv_t, wrp)


def _sample_in_proj_kernel(x_ref, g_ref, w_ref, q_ref, k_ref, v_ref, xr_ref, xp_ref):
    w = k_ref.shape[-1]
    u = _rms(x_ref[...], g_ref[...]).astype(BF16)
    p = jnp.dot(u, w_ref[...], preferred_element_type=F32)
    q_ref[...] = p[:, :w] * (HEAD_DIM ** -0.5)
    k_ref[...] = p[:, w:2 * w]
    v_ref[...] = p[:, 2 * w:3 * w]
    xr_ref[...] = p[:, 3 * w:4 * w]
    xp_ref[...] = p[:, 4 * w:5 * w]


def _sample_in_proj(x, g, w):
    rows = x.shape[0]
    width = w.shape[1] // 5
    return pl.pallas_call(
        _sample_in_proj_kernel,
        out_shape=[jax.ShapeDtypeStruct((rows, width), F32)] * 5,
        compiler_params=pltpu.CompilerParams(vmem_limit_bytes=VMEM_LIMIT_BYTES),
        name="sample_in_proj",
    )(x, g, w)


def _attn_kernel(bias_ref, q_ref, kv_ref, o_ref, acc_ref, after_ref, *, tile):
    pair = pl.program_id(1)
    i = pl.program_id(2)
    q = q_ref[0].astype(F32)
    first = lax.broadcasted_iota(jnp.int32, (tile, LANES), 1) < HEAD_DIM
    qh = (jnp.where(first, q, 0.0).astype(BF16), jnp.where(first, 0.0, q).astype(BF16))
    row = lax.broadcasted_iota(jnp.int32, (tile, tile), 0)
    col = lax.broadcasted_iota(jnp.int32, (tile, tile), 1)
    later = jnp.where(row > col, 1.0, 0.0).astype(BF16)
    causal = col < row
    acc_ref[...] = jnp.zeros_like(acc_ref)
    after_ref[...] = jnp.zeros_like(after_ref)

    def key_tile(j, diagonal):
        kt = kv_ref[0, 0, j]
        vt = kv_ref[1, 0, j]
        for h in range(HEADS_PER_BLOCK):
            s = jnp.dot(qh[h], kt, preferred_element_type=F32)
            log_beta, log_keep = _stick_logs(s + bias_ref[HEADS_PER_BLOCK * pair + h])
            if diagonal:
                log_keep = jnp.where(causal, log_keep, 0.0)
            hi, lo = _split_bf16(log_keep)
            inside = (jnp.dot(hi, later, preferred_element_type=F32)
                      + jnp.dot(lo, later, preferred_element_type=F32))
            after = after_ref[h]
            w = jnp.exp(log_beta + inside + jnp.concatenate([after] * (tile // LANES), axis=1))
            if diagonal:
                w = jnp.where(causal, w, 0.0)
            after_ref[h] = after + jnp.sum(log_keep, axis=1, keepdims=True)
            acc_ref[h] += lax.dot_general(w.astype(BF16), vt, _NT, preferred_element_type=F32)

    key_tile(i, True)

    def body(t, carry):
        key_tile(i - 1 - t, False)
        return carry

    lax.fori_loop(0, i, body, 0)
    o_ref[0] = jnp.where(first, acc_ref[0], acc_ref[1]).astype(o_ref.dtype)


def _prompt_attention(qb, kvb, bias):
    bsz, t, width = qb.shape
    n_tiles, tile = kvb.shape[2], kvb.shape[4]
    return pl.pallas_call(
        functools.partial(_attn_kernel, tile=tile),
        grid=(bsz, width // LANES, n_tiles),
        in_specs=[
            pl.BlockSpec(memory_space=pltpu.SMEM),
            pl.BlockSpec((1, tile, LANES), lambda b, p, i: (b, i, p)),
            pl.BlockSpec((2, 1, n_tiles, LANES, tile), lambda b, p, i: (0, b, 0, p, 0)),
        ],
        out_specs=pl.BlockSpec((1, tile, LANES), lambda b, p, i: (b, i, p)),
        out_shape=jax.ShapeDtypeStruct((bsz, t, width), BF16),
        scratch_shapes=[pltpu.VMEM((HEADS_PER_BLOCK, tile, LANES), F32)] * 2,
        compiler_params=_params("parallel", "parallel", "parallel"),
        name="prompt_attention",
    )(bias, qb, kvb)


def _decode_kernel(pt_ref, bias_ref, q_ref, *refs, n_pages):
    del pt_ref
    k_refs = refs[:n_pages]
    v_refs = refs[n_pages:2 * n_pages]
    o_ref, acc_ref, after_ref = refs[2 * n_pages:]
    g = pl.program_id(1)
    width, page = k_refs[0].shape
    heads = width // HEAD_DIM

    @pl.when(g == 0)
    def _():
        acc_ref[...] = jnp.zeros_like(acc_ref)
        after_ref[...] = jnp.zeros_like(after_ref)

    hrow = lax.broadcasted_iota(jnp.int32, (heads, page), 0)
    bias = jnp.zeros((heads, page), F32)
    for h in range(heads):
        bias = jnp.where(hrow == h, bias_ref[h], bias)
    row = lax.broadcasted_iota(jnp.int32, (page, 2 * page), 0)
    col = lax.broadcasted_iota(jnp.int32, (page, 2 * page), 1)
    later = jnp.where((row > col) | (col >= page), 1.0, 0.0).astype(BF16)

    log_beta, his, los = [], [], []
    for j in range(n_pages):
        s = jnp.zeros((heads, page), F32)
        for h in range(heads):
            rows = slice(h * HEAD_DIM, (h + 1) * HEAD_DIM)
            dot = jnp.sum(k_refs[j][rows, :] * q_ref[0, rows, :], axis=0, keepdims=True)
            s = jnp.where(hrow == h, dot, s)
        lb, lk = _stick_logs(s + bias)
        hi = lk.astype(BF16).astype(F32)
        log_beta.append(lb)
        his.append(hi)
        los.append(lk - hi)
    stacked = jnp.concatenate(his + los, axis=0).astype(BF16)
    sums = jnp.dot(stacked, later, preferred_element_type=F32)
    after = after_ref[...]
    weights = [None] * n_pages
    for j in reversed(range(n_pages)):
        both = sums[j * heads:(j + 1) * heads] + sums[(n_pages + j) * heads:(n_pages + j + 1) * heads]
        weights[j] = jnp.exp(log_beta[j] + both[:, :page] + after)
        after = after + both[:, page:]
    after_ref[...] = after
    for h in range(heads):
        rows = slice(h * HEAD_DIM, (h + 1) * HEAD_DIM)
        part = acc_ref[rows, :]
        for j in range(n_pages):
            part = part + v_refs[j][rows, :] * weights[j][h:h + 1, :]
        acc_ref[rows, :] = part

    @pl.when(g == pl.num_programs(1) - 1)
    def _():
        o_ref[0] = jnp.sum(acc_ref[...], axis=1, keepdims=True)


def _decode_attention(q_lanes, pool_k, pool_v, page_table, bias, layer_base, n_pages):
    bsz, width, page = q_lanes.shape
    n_steps = page_table.shape[1] // n_pages
    heads = width // HEAD_DIM

    def page_spec(j):
        def index(b, g, pt):
            return (layer_base + pt[b, (n_steps - 1 - g) * n_pages + j], 0, 0)
        return pl.BlockSpec((None, width, page), index)

    grid_spec = pltpu.PrefetchScalarGridSpec(
        num_scalar_prefetch=1,
        grid=(bsz, n_steps),
        in_specs=([pl.BlockSpec(memory_space=pltpu.SMEM),
                   pl.BlockSpec((1, width, page), lambda b, g, pt: (b, 0, 0))]
                  + [page_spec(j) for j in range(n_pages)] * 2),
        out_specs=pl.BlockSpec((1, width, 1), lambda b, g, pt: (b, 0, 0)),
        scratch_shapes=[pltpu.VMEM((width, page), F32), pltpu.VMEM((heads, page), F32)],
    )
    return pl.pallas_call(
        functools.partial(_decode_kernel, n_pages=n_pages),
        grid_spec=grid_spec,
        out_shape=jax.ShapeDtypeStruct((bsz, width, 1), F32),
        compiler_params=_params("parallel", "arbitrary"),
        name="decode_attention",
    )(page_table, bias, q_lanes, *([pool_k] * n_pages), *([pool_v] * n_pages))


def _shift_rows(x, s, fill):
    n, c = x.shape
    if s % SUBLANES == 0:
        return jnp.concatenate([jnp.full((s, c), fill, x.dtype), x[:n - s]], axis=0)
    rows = lax.broadcasted_iota(jnp.int32, x.shape, 0)
    return jnp.where(rows < s, fill, pltpu.roll(x, s, 0))


def _lru_gates(xc, wa_ref, ba_ref, wx_ref, bx_ref, neg_log_base):
    xb = xc.astype(BF16)
    r = jax.nn.sigmoid(jnp.dot(xb, wa_ref[...], preferred_element_type=F32) + ba_ref[...])
    gate = jax.nn.sigmoid(jnp.dot(xb, wx_ref[...], preferred_element_type=F32) + bx_ref[...])
    a = jnp.exp(-LRU_C * r * neg_log_base)
    return a, jnp.sqrt(1.0 - a * a), gate


def _pool_project(d_groups, pw_ref, ps_ref):
    y = [jnp.dot(d.astype(BF16), pw_ref[g], preferred_element_type=F32) for g, d in enumerate(d_groups)]
    return jnp.concatenate(y, axis=1) * ps_ref[...]


def _prompt_mixer_kernel(xr_ref, xp_ref, cw_ref, cb_ref, wa_ref, ba_ref, wx_ref, bx_ref, lam_ref,
                         pw_ref, ps_ref, ob_ref, oc_ref, hl_ref, *, chunk):
    t = xr_ref.shape[1]
    width = xr_ref.shape[2]
    group = width // len(POOL_WINDOWS)
    halo = 2 * SUBLANES
    neg_log_base = _softplus(-lam_ref[...])
    rows = lax.broadcasted_iota(jnp.int32, (chunk, 1), 0)

    def with_halo(ref, base, c):
        before = pl.multiple_of(jnp.maximum(base - halo, 0), SUBLANES)
        prev = jnp.where(c > 0, ref[0, pl.ds(before, halo), :], 0.0)
        return jnp.concatenate([prev, ref[0, pl.ds(base, chunk), :]], axis=0)

    def body(c, h_in):
        base = pl.multiple_of(c * chunk, chunk)
        pos = rows + base
        xs = with_halo(xr_ref, base, c)
        xc = cb_ref[...]
        for j in range(CONV_W):
            back = CONV_W - 1 - j
            tap = xs[halo:] if back == 0 else pltpu.roll(xs, back, 0)[halo:]
            xc = xc + tap * cw_ref[j:j + 1, :]
        a, mult, gate = _lru_gates(xc, wa_ref, ba_ref, wx_ref, bx_ref, neg_log_base)
        reset = pos == 0
        a = jnp.where(reset, 0.0, a)
        mult = jnp.where(reset, 1.0, mult)
        u = mult * gate * xc
        s = 1
        while s < chunk:
            a_prev = _shift_rows(a, s, 1.0)
            u_prev = _shift_rows(u, s, 0.0)
            u = u + a * u_prev
            a = a * a_prev
            s *= 2
        h = u + a * h_in
        ob_ref[0, pl.ds(base, chunk), :] = h.astype(ob_ref.dtype)
        ps = with_halo(xp_ref, base, c)
        diffs = []
        for g, win in enumerate(POOL_WINDOWS):
            x = ps[:, g * group:(g + 1) * group]
            total = x
            s = 1
            while s < win:
                total = total + _shift_rows(total, s, 0.0)
                s *= 2
            cnt = jnp.minimum(pos + 1, win).astype(F32)
            diffs.append(total[halo:] / cnt - x[halo:])
        oc_ref[0, pl.ds(base, chunk), :] = _pool_project(diffs, pw_ref, ps_ref).astype(oc_ref.dtype)
        return h[chunk - 1:chunk, :]

    h_last = lax.fori_loop(0, t // chunk, body, jnp.zeros((1, width), F32))
    hl_ref[0] = h_last


def _mixer_weights(lw):
    return (lw["conv_w"], lw["conv_b"], lw["gate_a_w"], lw["gate_a_b"], lw["gate_x_w"], lw["gate_x_b"],
            lw["lru_lambda"], lw["pool_w"], lw["pool_scale"])


def _prompt_mixers(xr, xp, lw, chunk):
    bsz, t, width = xr.shape
    seq = pl.BlockSpec((1, t, width), lambda b: (b, 0, 0))
    weights = _mixer_weights(lw)
    return pl.pallas_call(
        functools.partial(_prompt_mixer_kernel, chunk=chunk),
        grid=(bsz,),
        in_specs=[seq, seq] + [_resident(w.shape) for w in weights],
        out_specs=[seq, seq, pl.BlockSpec((1, 1, width), lambda b: (b, 0, 0))],
        out_shape=[jax.ShapeDtypeStruct((bsz, t, width), BF16), jax.ShapeDtypeStruct((bsz, t, width), BF16),
                   jax.ShapeDtypeStruct((bsz, 1, width), F32)],
        compiler_params=_params("parallel"),
        name="prompt_mixers",
    )(xr, xp, *weights)


def _sample_mixer_kernel(xr_ref, xp_ref, ch_ref, ph_ref, h0_ref, cw_ref, cb_ref, wa_ref, ba_ref, wx_ref, bx_ref,
                         lam_ref, pw_ref, ps_ref, ob_ref, oc_ref, h_ref):
    width = xr_ref.shape[1]
    group = width // len(POOL_WINDOWS)
    xc = cb_ref[...]
    for j in range(CONV_W - 1):
        xc = xc + ch_ref[j] * cw_ref[j:j + 1, :]
    xc = xc + xr_ref[...] * cw_ref[CONV_W - 1:CONV_W, :]
    a, mult, gate = _lru_gates(xc, wa_ref, ba_ref, wx_ref, bx_ref, _softplus(-lam_ref[...]))
    h = mult * gate * xc + a * h0_ref[...]
    h_ref[...] = h
    ob_ref[...] = h.astype(ob_ref.dtype)
    n_hist = ph_ref.shape[0]
    diffs = []
    for g, win in enumerate(POOL_WINDOWS):
        cols = slice(g * group, (g + 1) * group)
        x = xp_ref[:, cols]
        total = x
        for back in range(1, win):
            total = total + ph_ref[n_hist - back][:, cols]
        diffs.append(total / float(win) - x)
    oc_ref[...] = _pool_project(diffs, pw_ref, ps_ref).astype(oc_ref.dtype)


def _sample_mixers(xr, xp, conv_hist, pool_hist, h0, lw):
    bsz, width = xr.shape
    return pl.pallas_call(
        _sample_mixer_kernel,
        out_shape=[jax.ShapeDtypeStruct((bsz, width), BF16), jax.ShapeDtypeStruct((bsz, width), BF16),
                   jax.ShapeDtypeStruct((bsz, width), F32)],
        compiler_params=pltpu.CompilerParams(vmem_limit_bytes=VMEM_LIMIT_BYTES),
        name="sample_mixers",
    )(xr, xp, conv_hist, pool_hist, h0, *_mixer_weights(lw))


def _merge_mlp_kernel(x_ref, oa_ref, ob_ref, oc_ref, g_pre_ref, wg_ref, wb_ref, wo_ref, g_post_ref,
                      g_mlp_pre_ref, wu_ref, wd_ref, g_mlp_post_ref, y_ref):
    d = x_ref.shape[1]
    x = x_ref[...]
    u = _rms(x, g_pre_ref[...]).astype(BF16)
    mixed = None
    for n, branch in enumerate((oa_ref, ob_ref, oc_ref)):
        gate = jax.nn.sigmoid(jnp.dot(u, wg_ref[:, n * d:(n + 1) * d], preferred_element_type=F32))
        part = gate * jnp.dot(branch[...], wb_ref[n], preferred_element_type=F32)
        mixed = part if mixed is None else mixed + part
    mix_out = jnp.dot(mixed.astype(BF16), wo_ref[...], preferred_element_type=F32)
    x = x + _rms(mix_out, g_post_ref[...])
    u2 = _rms(x, g_mlp_pre_ref[...]).astype(BF16)
    hid = jnp.square(jnp.maximum(jnp.dot(u2, wu_ref[...], preferred_element_type=F32), 0.0))
    ffn = jnp.dot(hid.astype(BF16), wd_ref[...], preferred_element_type=F32)
    y_ref[...] = x + _rms(ffn, g_mlp_post_ref[...])


def _merge_mlp(x, oa, ob, oc, lw, tm):
    rows, d = x.shape
    width = oa.shape[1]
    weights = (lw["norm_mix_pre"], lw["w_gate"], lw["w_branch"], lw["w_out"], lw["norm_mix_post"],
               lw["norm_mlp_pre"], lw["w_up"], lw["w_down"], lw["norm_mlp_post"])
    row = pl.BlockSpec((tm, d), lambda i: (i, 0))
    branch = pl.BlockSpec((tm, width), lambda i: (i, 0))
    return pl.pallas_call(
        _merge_mlp_kernel,
        grid=(rows // tm,),
        in_specs=[row, branch, branch, branch] + [_resident(w.shape) for w in weights],
        out_specs=row,
        out_shape=jax.ShapeDtypeStruct((rows, d), F32),
        compiler_params=_params("parallel"),
        name="merge_mlp",
    )(x, oa, ob, oc, *weights)


def _block_diag(w):
    n, c, _ = w.shape
    eye = jnp.eye(n, dtype=w.dtype)
    return jnp.einsum("nij,nm->nimj", w, eye).reshape(n * c, n * c)


def _layer_weights(l, width, p):
    row = lambda a: a[l].reshape(1, -1)
    w_in = p["w_in"][l].astype(BF16)
    return {
        "norm_mix_pre": row(p["norm_mix_pre"]), "norm_mix_post": row(p["norm_mix_post"]),
        "norm_mlp_pre": row(p["norm_mlp_pre"]), "norm_mlp_post": row(p["norm_mlp_post"]),
        "w_seq": w_in[:, :5 * width], "w_q": w_in[:, :width], "w_kv_t": w_in[:, width:3 * width].T,
        "w_rp": w_in[:, 3 * width:5 * width], "w_gate": w_in[:, 5 * width:],
        "sb_bias": p["sb_bias"][l],
        "conv_w": p["conv_w"][l], "conv_b": row(p["conv_b"]),
        "gate_a_w": _block_diag(p["gate_a_w"][l]).astype(BF16), "gate_a_b": row(p["gate_a_b"]),
        "gate_x_w": _block_diag(p["gate_x_w"][l]).astype(BF16), "gate_x_b": row(p["gate_x_b"]),
        "lru_lambda": row(p["lru_lambda"]),
        "pool_w": p["pool_w"][l].astype(BF16), "pool_scale": row(p["pool_scale"]),
        "w_branch": p["w_branch"][l].astype(BF16), "w_out": p["w_out"][l].astype(BF16),
        "w_up": p["w_up"][l].astype(BF16), "w_down": p["w_down"][l].astype(BF16),
    }


def kernel(x_prompt, x_sample, cache_k, cache_v, page_table, state_h, state_conv, state_pool, norm_mix_pre, norm_mix_post, norm_mlp_pre, norm_mlp_post, w_in, sb_bias, conv_w, conv_b, gate_a_w, gate_a_b, gate_x_w, gate_x_b, lru_lambda, pool_w, pool_scale, w_branch, w_out, w_up, w_down):
    params = dict(norm_mix_pre=norm_mix_pre, norm_mix_post=norm_mix_post, norm_mlp_pre=norm_mlp_pre,
                  norm_mlp_post=norm_mlp_post, w_in=w_in, sb_bias=sb_bias, conv_w=conv_w, conv_b=conv_b,
                  gate_a_w=gate_a_w, gate_a_b=gate_a_b, gate_x_w=gate_x_w, gate_x_b=gate_x_b,
                  lru_lambda=lru_lambda, pool_w=pool_w, pool_scale=pool_scale, w_branch=w_branch,
                  w_out=w_out, w_up=w_up, w_down=w_down)
    b_p, seq, d = x_prompt.shape
    b_s, dec_seq, _ = x_sample.shape
    assert dec_seq == 1, "the sample group decodes one token per sequence"
    depth, n_phys, page, heads, head_dim = cache_k.shape
    assert head_dim == HEAD_DIM
    width = heads * head_dim
    n_conv, n_pool = CONV_W - 1, max(POOL_WINDOWS) - 1
    assert page_table.shape[1] * page > n_pool, "sample positions must have complete conv / pooling windows"
    pool_k = jnp.transpose(cache_k, (0, 1, 3, 4, 2)).reshape(depth * n_phys, width, page)
    pool_v = jnp.transpose(cache_v, (0, 1, 3, 4, 2)).reshape(depth * n_phys, width, page)

    def rows_to_heads(kv_t):
        kv = jnp.transpose(kv_t.reshape(2, b_p, heads, head_dim, seq), (0, 1, 4, 2, 3))
        return kv[0], kv[1]

    yp = x_prompt
    ys = x_sample.reshape(b_s, d)
    outs = {name: [] for name in ("kp", "vp", "ks", "vs", "hp", "hs", "cp", "cs", "pp", "ps")}
    for l in range(depth):
        lw = _layer_weights(l, width, params)
        qb, kv_t, kvb, xr, xp = _prompt_in_proj(yp, lw["norm_mix_pre"], lw["w_q"], lw["w_kv_t"], lw["w_rp"], ATT_TILE)
        oa = _prompt_attention(qb, kvb, lw["sb_bias"])
        ob, oc, h_last = _prompt_mixers(xr, xp, lw, SEQ_CHUNK)
        flat = lambda a: a.reshape(b_p * seq, -1)
        yp = _merge_mlp(flat(yp), flat(oa), flat(ob), flat(oc), lw, ROW_TILE).reshape(b_p, seq, d)
        k, v = rows_to_heads(kv_t)
        outs["kp"].append(k)
        outs["vp"].append(v)
        outs["hp"].append(h_last.reshape(b_p, width))
        outs["cp"].append(xr[:, seq - n_conv:])
        outs["pp"].append(xp[:, seq - n_pool:])
        q, k, v, xr, xp = _sample_in_proj(ys, lw["norm_mix_pre"], lw["w_seq"])
        q_lanes = jnp.broadcast_to(q[:, :, None], (b_s, width, page))
        oa = _decode_attention(q_lanes, pool_k, pool_v, page_table, lw["sb_bias"], l * n_phys, PAGES_PER_STEP)
        ob, oc, h_new = _sample_mixers(xr, xp, jnp.swapaxes(state_conv[l], 0, 1), jnp.swapaxes(state_pool[l], 0, 1),
                                       state_h[l], lw)
        ys = _merge_mlp(ys, oa.reshape(b_s, width).astype(BF16), ob, oc, lw, b_s)
        outs["ks"].append(k.reshape(b_s, 1, heads, head_dim))
        outs["vs"].append(v.reshape(b_s, 1, heads, head_dim))
        outs["hs"].append(h_new)
        outs["cs"].append(jnp.concatenate([state_conv[l][:, 1:], xr[:, None, :]], axis=1))
        outs["ps"].append(jnp.concatenate([state_pool[l][:, 1:], xp[:, None, :]], axis=1))

    stack = lambda name: jnp.stack(outs[name])
    return (yp, ys.reshape(b_s, 1, d), stack("kp"), stack("vp"), stack("ks"), stack("vs"),
            stack("hp"), stack("hs"), stack("cp"), stack("cs"), stack("pp"), stack("ps"))
```

```python
import functools

import jax
import jax.numpy as jnp
from jax import lax
from jax.experimental import pallas as pl
from jax.experimental.pallas import tpu as pltpu

F32 = jnp.float32
BF16 = jnp.bfloat16

HEAD_DIM = 64
LRU_C = 8.0
CONV_W = 4
POOL_WINDOWS = (2, 4, 8, 16)
RMS_EPS = 1e-6

LANES = 128
SUBLANES = 8
HEADS_PER_BLOCK = LANES // HEAD_DIM
VMEM_LIMIT_BYTES = 56 * 1024 * 1024

ROW_TILE = 256
ATT_TILE = 256
ATT_ROWS = 32
SEQ_CHUNK = 256
PAGES_PER_STEP = 8

_NT = (((1,), (1,)), ((), ()))


def _params(*sem):
    return pltpu.CompilerParams(dimension_semantics=sem, vmem_limit_bytes=VMEM_LIMIT_BYTES)


def _resident(shape):
    zeros = (0,) * len(shape)
    return pl.BlockSpec(shape, lambda *_: zeros, pipeline_mode=pl.Buffered(1))


def _rms(x, g):
    return x * lax.rsqrt(jnp.mean(x * x, axis=-1, keepdims=True) + RMS_EPS) * g


def _softplus(x):
    return jnp.maximum(x, 0.0) + jnp.log1p(jnp.exp(-jnp.abs(x)))


def _stick_logs(z):
    sign = jnp.uint32(1 << 31)
    neg_abs = lax.bitcast_convert_type(lax.bitcast_convert_type(z, jnp.uint32) | sign, F32)
    log_beta = jnp.minimum(z, 0.0) - jnp.log(1.0 + jnp.exp(neg_abs))
    return log_beta, log_beta - z


def _split_bf16(x):
    hi = x.astype(BF16)
    lo = (x - hi.astype(F32)).astype(BF16)
    return hi, lo


def _prompt_in_proj_kernel(x_ref, g_ref, wq_ref, wkv_ref, wrp_ref, *refs):
    qb_ref, k_ref, v_ref, kvb_ref, xr_ref, xp_ref = refs[-6:]
    w = qb_ref.shape[-1]
    u = _rms(x_ref[0], g_ref[...]).astype(BF16)
    qb_ref[0] = (jnp.dot(u, wq_ref[...], preferred_element_type=F32) * (HEAD_DIM ** -0.5)).astype(BF16)
    kv = lax.dot_general(wkv_ref[...], u, _NT, preferred_element_type=F32)
    k_ref[0, 0] = kv[:w]
    v_ref[0, 0] = kv[w:]
    kvb_ref[:, 0, 0] = kv.reshape(2, w, kv.shape[1]).astype(BF16)
    rp = jnp.dot(u, wrp_ref[...], preferred_element_type=F32)
    xr_ref[0] = rp[:, :w]
    xp_ref[0] = rp[:, w:]


def _prompt_in_proj(x, g, wq, wkv_t, wrp, tm, layer, depth, kv_stacks):
    bsz, t, d = x.shape
    width = wq.shape[1]
    rows = pl.BlockSpec((1, tm, width), lambda b, i: (b, i, 0))
    stack = pl.BlockSpec((1, 1, width, tm), lambda b, i: (layer, b, 0, i))
    n_in = 5
    return pl.pallas_call(
        _prompt_in_proj_kernel,
        grid=(bsz, t // tm),
        in_specs=[pl.BlockSpec((1, tm, d), lambda b, i: (b, i, 0)), _resident((1, d)),
                  _resident(wq.shape), _resident(wkv_t.shape), _resident(wrp.shape)]
                 + [pl.BlockSpec(memory_space=pl.ANY)] * len(kv_stacks),
        out_specs=[rows, stack, stack,
                   pl.BlockSpec((2, 1, 1, width, tm), lambda b, i: (0, b, i, 0, 0)),
                   rows, rows],
        out_shape=[jax.ShapeDtypeStruct((bsz, t, width), BF16),
                   jax.ShapeDtypeStruct((depth, bsz, width, t), F32),
                   jax.ShapeDtypeStruct((depth, bsz, width, t), F32),
                   jax.ShapeDtypeStruct((2, bsz, t // tm, width, tm), BF16),
                   jax.ShapeDtypeStruct((bsz, t, width), F32),
                   jax.ShapeDtypeStruct((bsz, t, width), F32)],
        input_output_aliases={n_in + n: 1 + n for n in range(len(kv_stacks))},
        compiler_params=_params("parallel", "parallel"),
        name="prompt_in_proj",
    )(x, g, wq, wkv_t, wrp, *kv_stacks)


def _sample_in_proj_kernel(x_ref, g_ref, w_ref, q_ref, k_ref, v_ref, xr_ref, xp_ref):
    w = k_ref.shape[-1]
    u = _rms(x_ref[...], g_ref[...]).astype(BF16)
    p = jnp.dot(u, w_ref[...], preferred_element_type=F32)
    q_ref[...] = p[:, :w] * (HEAD_DIM ** -0.5)
    k_ref[...] = p[:, w:2 * w]
    v_ref[...] = p[:, 2 * w:3 * w]
    xr_ref[...] = p[:, 3 * w:4 * w]
    xp_ref[...] = p[:, 4 * w:5 * w]


def _sample_in_proj(x, g, w):
    rows = x.shape[0]
    width = w.shape[1] // 5
    return pl.pallas_call(
        _sample_in_proj_kernel,
        out_shape=[jax.ShapeDtypeStruct((rows, width), F32)] * 5,
        compiler_params=pltpu.CompilerParams(vmem_limit_bytes=VMEM_LIMIT_BYTES),
        name="sample_in_proj",
    )(x, g, w)


def _attn_kernel(bias_ref, q_ref, kv_ref, o_ref, qh_ref, later_ref, s_ref, lb_ref, split_ref, in_ref, w_ref,
                 sum_ref, after_ref, acc_ref, *, tile, rows):
    pair = pl.program_id(1)
    i = pl.program_id(2)
    q = q_ref[0].astype(F32)
    first = lax.broadcasted_iota(jnp.int32, (tile, LANES), 1) < HEAD_DIM
    qh_ref[0] = jnp.where(first, q, 0.0).astype(BF16)
    qh_ref[1] = jnp.where(first, 0.0, q).astype(BF16)
    row = lax.broadcasted_iota(jnp.int32, (tile, tile), 0)
    col = lax.broadcasted_iota(jnp.int32, (tile, tile), 1)
    later = jnp.where(row > col, 1.0, 0.0).astype(BF16)
    later_ref[:tile] = later
    later_ref[tile:] = later
    acc_ref[...] = jnp.zeros_like(acc_ref)
    after_ref[...] = jnp.zeros_like(after_ref)
    chunks = [slice(r, r + rows) for r in range(0, tile, rows)]

    def logs_pass(h, diagonal):
        bias = bias_ref[HEADS_PER_BLOCK * pair + h]
        for c in chunks:
            log_beta, log_keep = _stick_logs(s_ref[h, c, :] + bias)
            if diagonal:
                causal = (lax.broadcasted_iota(jnp.int32, (rows, tile), 1)
                          < lax.broadcasted_iota(jnp.int32, (rows, tile), 0) + c.start)
                log_keep = jnp.where(causal, log_keep, 0.0)
            hi, lo = _split_bf16(log_keep)
            lb_ref[h, c, :] = log_beta
            split_ref[h, c, :tile] = hi
            split_ref[h, c, tile:] = lo
            sum_ref[h, c, :] = jnp.broadcast_to(jnp.sum(log_keep, axis=1, keepdims=True), (rows, LANES))

    def weights_pass(h, diagonal):
        for c in chunks:
            after = after_ref[h, c, :]
            w = jnp.exp(lb_ref[h, c, :] + in_ref[h, c, :] + jnp.concatenate([after] * (tile // LANES), axis=1))
            if diagonal:
                causal = (lax.broadcasted_iota(jnp.int32, (rows, tile), 1)
                          < lax.broadcasted_iota(jnp.int32, (rows, tile), 0) + c.start)
                w = jnp.where(causal, w, 0.0)
            w_ref[h, c, :] = w.astype(BF16)
            after_ref[h, c, :] = after + sum_ref[h, c, :]

    def scores(j, h):
        s_ref[h] = jnp.dot(qh_ref[h], kv_ref[0, 0, j], preferred_element_type=F32)

    def values(j):
        vt = kv_ref[1, 0, j]
        for h in range(HEADS_PER_BLOCK):
            acc_ref[h] += lax.dot_general(w_ref[h], vt, _NT, preferred_element_type=F32)

    def key_tile(j, diagonal):
        nxt = jnp.maximum(j - 1, 0)
        for h in range(HEADS_PER_BLOCK):
            logs_pass(h, diagonal)
            scores(nxt, h)
            in_ref[h] = jnp.dot(split_ref[h], later_ref[...], preferred_element_type=F32)
        for h in range(HEADS_PER_BLOCK):
            weights_pass(h, diagonal)

    for h in range(HEADS_PER_BLOCK):
        scores(i, h)
    key_tile(i, True)

    def body(t, carry):
        j = i - 1 - t
        values(j + 1)
        key_tile(j, False)
        return carry

    lax.fori_loop(0, i, body, 0)
    values(0)
    o_ref[0] = jnp.where(first, acc_ref[0], acc_ref[1]).astype(o_ref.dtype)


def _prompt_attention(qb, kvb, bias):
    bsz, t, width = qb.shape
    n_tiles, tile = kvb.shape[2], kvb.shape[4]
    per_head = lambda cols, dt: pltpu.VMEM((HEADS_PER_BLOCK, tile, cols), dt)
    return pl.pallas_call(
        functools.partial(_attn_kernel, tile=tile, rows=ATT_ROWS),
        grid=(bsz, width // LANES, n_tiles),
        in_specs=[
            pl.BlockSpec(memory_space=pltpu.SMEM),
            pl.BlockSpec((1, tile, LANES), lambda b, p, i: (b, i, p)),
            pl.BlockSpec((2, 1, n_tiles, LANES, tile), lambda b, p, i: (0, b, 0, p, 0)),
        ],
        out_specs=pl.BlockSpec((1, tile, LANES), lambda b, p, i: (b, i, p)),
        out_shape=jax.ShapeDtypeStruct((bsz, t, width), BF16),
        scratch_shapes=[per_head(LANES, BF16), pltpu.VMEM((2 * tile, tile), BF16), per_head(tile, F32),
                        per_head(tile, F32), per_head(2 * tile, BF16), per_head(tile, F32), per_head(tile, BF16),
                        per_head(LANES, F32), per_head(LANES, F32), per_head(LANES, F32)],
        compiler_params=_params("parallel", "parallel", "parallel"),
        name="prompt_attention",
    )(bias, qb, kvb)


def _decode_kernel(pt_ref, bias_ref, q_ref, *refs, n_pages):
    del pt_ref
    k_refs = refs[:n_pages]
    v_refs = refs[n_pages:2 * n_pages]
    o_ref, acc_ref, after_ref = refs[2 * n_pages:]
    g = pl.program_id(1)
    width, page = k_refs[0].shape
    heads = width // HEAD_DIM

    @pl.when(g == 0)
    def _():
        acc_ref[...] = jnp.zeros_like(acc_ref)
        after_ref[...] = jnp.zeros_like(after_ref)

    hrow = lax.broadcasted_iota(jnp.int32, (heads, page), 0)
    bias = jnp.zeros((heads, page), F32)
    for h in range(heads):
        bias = jnp.where(hrow == h, bias_ref[h], bias)
    row = lax.broadcasted_iota(jnp.int32, (page, 2 * page), 0)
    col = lax.broadcasted_iota(jnp.int32, (page, 2 * page), 1)
    later = jnp.where((row > col) | (col >= page), 1.0, 0.0).astype(BF16)

    log_beta, his, los = [], [], []
    for j in range(n_pages):
        s = jnp.zeros((heads, page), F32)
        for h in range(heads):
            rows = slice(h * HEAD_DIM, (h + 1) * HEAD_DIM)
            dot = jnp.sum(k_refs[j][rows, :] * q_ref[0, rows, :], axis=0, keepdims=True)
            s = jnp.where(hrow == h, dot, s)
        lb, lk = _stick_logs(s + bias)
        hi = lk.astype(BF16).astype(F32)
        log_beta.append(lb)
        his.append(hi)
        los.append(lk - hi)
    stacked = jnp.concatenate(his + los, axis=0).astype(BF16)
    sums = jnp.dot(stacked, later, preferred_element_type=F32)
    after = after_ref[...]
    weights = [None] * n_pages
    for j in reversed(range(n_pages)):
        both = sums[j * heads:(j + 1) * heads] + sums[(n_pages + j) * heads:(n_pages + j + 1) * heads]
        weights[j] = jnp.exp(log_beta[j] + both[:, :page] + after)
        after = after + both[:, page:]
    after_ref[...] = after
    for h in range(heads):
        rows = slice(h * HEAD_DIM, (h + 1) * HEAD_DIM)
        part = acc_ref[rows, :]
        for j in range(n_pages):
            part = part + v_refs[j][rows, :] * weights[j][h:h + 1, :]
        acc_ref[rows, :] = part

    @pl.when(g == pl.num_programs(1) - 1)
    def _():
        o_ref[0] = jnp.sum(acc_ref[...], axis=1, keepdims=True)


def _decode_attention(q_lanes, pool_k, pool_v, page_table, bias, layer_base, n_pages):
    bsz, width, page = q_lanes.shape
    n_steps = page_table.shape[1] // n_pages
    heads = width // HEAD_DIM

    def page_spec(j):
        def index(b, g, pt):
            return (layer_base + pt[b, (n_steps - 1 - g) * n_pages + j], 0, 0)
        return pl.BlockSpec((None, width, page), index)

    grid_spec = pltpu.PrefetchScalarGridSpec(
        num_scalar_prefetch=1,
        grid=(bsz, n_steps),
        in_specs=([pl.BlockSpec(memory_space=pltpu.SMEM),
                   pl.BlockSpec((1, width, page), lambda b, g, pt: (b, 0, 0))]
                  + [page_spec(j) for j in range(n_pages)] * 2),
        out_specs=pl.BlockSpec((1, width, 1), lambda b, g, pt: (b, 0, 0)),
        scratch_shapes=[pltpu.VMEM((width, page), F32), pltpu.VMEM((heads, page), F32)],
    )
    return pl.pallas_call(
        functools.partial(_decode_kernel, n_pages=n_pages),
        grid_spec=grid_spec,
        out_shape=jax.ShapeDtypeStruct((bsz, width, 1), F32),
        compiler_params=_params("parallel", "arbitrary"),
        name="decode_attention",
    )(page_table, bias, q_lanes, *([pool_k] * n_pages), *([pool_v] * n_pages))


def _shift_rows(x, s, fill):
    n, c = x.shape
    if s % SUBLANES == 0:
        return jnp.concatenate([jnp.full((s, c), fill, x.dtype), x[:n - s]], axis=0)
    rows = lax.broadcasted_iota(jnp.int32, x.shape, 0)
    return jnp.where(rows < s, fill, pltpu.roll(x, s, 0))


def _lru_gates(xc, wa_ref, ba_ref, wx_ref, bx_ref, neg_log_base):
    xb = xc.astype(BF16)
    r = jax.nn.sigmoid(jnp.dot(xb, wa_ref[...], preferred_element_type=F32) + ba_ref[...])
    gate = jax.nn.sigmoid(jnp.dot(xb, wx_ref[...], preferred_element_type=F32) + bx_ref[...])
    a = jnp.exp(-LRU_C * r * neg_log_base)
    return a, jnp.sqrt(1.0 - a * a), gate


def _pool_project(d_groups, pw_ref, ps_ref):
    y = [jnp.dot(d.astype(BF16), pw_ref[g], preferred_element_type=F32) for g, d in enumerate(d_groups)]
    return jnp.concatenate(y, axis=1) * ps_ref[...]


def _prompt_mixer_kernel(xr_ref, xp_ref, cw_ref, cb_ref, wa_ref, ba_ref, wx_ref, bx_ref, lam_ref,
                         pw_ref, ps_ref, ob_ref, oc_ref, hl_ref, *, chunk):
    t = xr_ref.shape[1]
    width = xr_ref.shape[2]
    group = width // len(POOL_WINDOWS)
    halo = 2 * SUBLANES
    neg_log_base = _softplus(-lam_ref[...])
    rows = lax.broadcasted_iota(jnp.int32, (chunk, 1), 0)

    def with_halo(ref, base, c):
        before = pl.multiple_of(jnp.maximum(base - halo, 0), SUBLANES)
        prev = jnp.where(c > 0, ref[0, pl.ds(before, halo), :], 0.0)
        return jnp.concatenate([prev, ref[0, pl.ds(base, chunk), :]], axis=0)

    def body(c, h_in):
        base = pl.multiple_of(c * chunk, chunk)
        pos = rows + base
        xs = with_halo(xr_ref, base, c)
        xc = cb_ref[...]
        for j in range(CONV_W):
            back = CONV_W - 1 - j
            tap = xs[halo:] if back == 0 else pltpu.roll(xs, back, 0)[halo:]
            xc = xc + tap * cw_ref[j:j + 1, :]
        a, mult, gate = _lru_gates(xc, wa_ref, ba_ref, wx_ref, bx_ref, neg_log_base)
        reset = pos == 0
        a = jnp.where(reset, 0.0, a)
        mult = jnp.where(reset, 1.0, mult)
        u = mult * gate * xc
        s = 1
        while s < chunk:
            a_prev = _shift_rows(a, s, 1.0)
            u_prev = _shift_rows(u, s, 0.0)
            u = u + a * u_prev
            a = a * a_prev
            s *= 2
        h = u + a * h_in
        ob_ref[0, pl.ds(base, chunk), :] = h.astype(ob_ref.dtype)
        ps = with_halo(xp_ref, base, c)
        diffs = []
        for g, win in enumerate(POOL_WINDOWS):
            x = ps[:, g * group:(g + 1) * group]
            total = x
            s = 1
            while s < win:
                total = total + _shift_rows(total, s, 0.0)
                s *= 2
            cnt = jnp.minimum(pos + 1, win).astype(F32)
            diffs.append(total[halo:] / cnt - x[halo:])
        oc_ref[0, pl.ds(base, chunk), :] = _pool_project(diffs, pw_ref, ps_ref).astype(oc_ref.dtype)
        return h[chunk - 1:chunk, :]

    h_last = lax.fori_loop(0, t // chunk, body, jnp.zeros((1, width), F32))
    hl_ref[0] = h_last


def _mixer_weights(lw):
    return (lw["conv_w"], lw["conv_b"], lw["gate_a_w"], lw["gate_a_b"], lw["gate_x_w"], lw["gate_x_b"],
            lw["lru_lambda"], lw["pool_w"], lw["pool_scale"])


def _prompt_mixers(xr, xp, lw, chunk):
    bsz, t, width = xr.shape
    seq = pl.BlockSpec((1, t, width), lambda b: (b, 0, 0))
    weights = _mixer_weights(lw)
    return pl.pallas_call(
        functools.partial(_prompt_mixer_kernel, chunk=chunk),
        grid=(bsz,),
        in_specs=[seq, seq] + [_resident(w.shape) for w in weights],
        out_specs=[seq, seq, pl.BlockSpec((1, 1, width), lambda b: (b, 0, 0))],
        out_shape=[jax.ShapeDtypeStruct((bsz, t, width), BF16), jax.ShapeDtypeStruct((bsz, t, width), BF16),
                   jax.ShapeDtypeStruct((bsz, 1, width), F32)],
        compiler_params=_params("parallel"),
        name="prompt_mixers",
    )(xr, xp, *weights)


def _sample_mixer_kernel(xr_ref, xp_ref, ch_ref, ph_ref, h0_ref, cw_ref, cb_ref, wa_ref, ba_ref, wx_ref, bx_ref,
                         lam_ref, pw_ref, ps_ref, ob_ref, oc_ref, h_ref):
    width = xr_ref.shape[1]
    group = width // len(POOL_WINDOWS)
    xc = cb_ref[...]
    for j in range(CONV_W - 1):
        xc = xc + ch_ref[j] * cw_ref[j:j + 1, :]
    xc = xc + xr_ref[...] * cw_ref[CONV_W - 1:CONV_W, :]
    a, mult, gate = _lru_gates(xc, wa_ref, ba_ref, wx_ref, bx_ref, _softplus(-lam_ref[...]))
    h = mult * gate * xc + a * h0_ref[...]
    h_ref[...] = h
    ob_ref[...] = h.astype(ob_ref.dtype)
    n_hist = ph_ref.shape[0]
    diffs = []
    for g, win in enumerate(POOL_WINDOWS):
        cols = slice(g * group, (g + 1) * group)
        x = xp_ref[:, cols]
        total = x
        for back in range(1, win):
            total = total + ph_ref[n_hist - back][:, cols]
        diffs.append(total / float(win) - x)
    oc_ref[...] = _pool_project(diffs, pw_ref, ps_ref).astype(oc_ref.dtype)


def _sample_mixers(xr, xp, conv_hist, pool_hist, h0, lw):
    bsz, width = xr.shape
    return pl.pallas_call(
        _sample_mixer_kernel,
        out_shape=[jax.ShapeDtypeStruct((bsz, width), BF16), jax.ShapeDtypeStruct((bsz, width), BF16),
                   jax.ShapeDtypeStruct((bsz, width), F32)],
        compiler_params=pltpu.CompilerParams(vmem_limit_bytes=VMEM_LIMIT_BYTES),
        name="sample_mixers",
    )(xr, xp, conv_hist, pool_hist, h0, *_mixer_weights(lw))


def _merge_mlp_kernel(x_ref, oa_ref, ob_ref, oc_ref, g_pre_ref, wg_ref, wb_ref, wo_ref, g_post_ref,
                      g_mlp_pre_ref, wu_ref, wd_ref, g_mlp_post_ref, y_ref):
    d = x_ref.shape[1]
    x = x_ref[...]
    u = _rms(x, g_pre_ref[...]).astype(BF16)
    mixed = None
    for n, branch in enumerate((oa_ref, ob_ref, oc_ref)):
        gate = jax.nn.sigmoid(jnp.dot(u, wg_ref[:, n * d:(n + 1) * d], preferred_element_type=F32))
        part = gate * jnp.dot(branch[...], wb_ref[n], preferred_element_type=F32)
        mixed = part if mixed is None else mixed + part
    mix_out = jnp.dot(mixed.astype(BF16), wo_ref[...], preferred_element_type=F32)
    x = x + _rms(mix_out, g_post_ref[...])
    u2 = _rms(x, g_mlp_pre_ref[...]).astype(BF16)
    hid = jnp.square(jnp.maximum(jnp.dot(u2, wu_ref[...], preferred_element_type=F32), 0.0))
    ffn = jnp.dot(hid.astype(BF16), wd_ref[...], preferred_element_type=F32)
    y_ref[...] = x + _rms(ffn, g_mlp_post_ref[...])


def _merge_mlp(x, oa, ob, oc, lw, tm):
    rows, d = x.shape
    width = oa.shape[1]
    weights = (lw["norm_mix_pre"], lw["w_gate"], lw["w_branch"], lw["w_out"], lw["norm_mix_post"],
               lw["norm_mlp_pre"], lw["w_up"], lw["w_down"], lw["norm_mlp_post"])
    row = pl.BlockSpec((tm, d), lambda i: (i, 0))
    branch = pl.BlockSpec((tm, width), lambda i: (i, 0))
    return pl.pallas_call(
        _merge_mlp_kernel,
        grid=(rows // tm,),
        in_specs=[row, branch, branch, branch] + [_resident(w.shape) for w in weights],
        out_specs=row,
        out_shape=jax.ShapeDtypeStruct((rows, d), F32),
        compiler_params=_params("parallel"),
        name="merge_mlp",
    )(x, oa, ob, oc, *weights)


def _block_diag(w):
    n, c, _ = w.shape
    eye = jnp.eye(n, dtype=w.dtype)
    return jnp.einsum("nij,nm->nimj", w, eye).reshape(n * c, n * c)


def _layer_weights(l, width, p):
    row = lambda a: a[l].reshape(1, -1)
    w_in = p["w_in"][l].astype(BF16)
    return {
        "norm_mix_pre": row(p["norm_mix_pre"]), "norm_mix_post": row(p["norm_mix_post"]),
        "norm_mlp_pre": row(p["norm_mlp_pre"]), "norm_mlp_post": row(p["norm_mlp_post"]),
        "w_seq": w_in[:, :5 * width], "w_q": w_in[:, :width], "w_kv_t": w_in[:, width:3 * width].T,
        "w_rp": w_in[:, 3 * width:5 * width], "w_gate": w_in[:, 5 * width:],
        "sb_bias": p["sb_bias"][l],
        "conv_w": p["conv_w"][l], "conv_b": row(p["conv_b"]),
        "gate_a_w": _block_diag(p["gate_a_w"][l]).astype(BF16), "gate_a_b": row(p["gate_a_b"]),
        "gate_x_w": _block_diag(p["gate_x_w"][l]).astype(BF16), "gate_x_b": row(p["gate_x_b"]),
        "lru_lambda": row(p["lru_lambda"]),
        "pool_w": p["pool_w"][l].astype(BF16), "pool_scale": row(p["pool_scale"]),
        "w_branch": p["w_branch"][l].astype(BF16), "w_out": p["w_out"][l].astype(BF16),
        "w_up": p["w_up"][l].astype(BF16), "w_down": p["w_down"][l].astype(BF16),
    }


def kernel(x_prompt, x_sample, cache_k, cache_v, page_table, state_h, state_conv, state_pool, norm_mix_pre, norm_mix_post, norm_mlp_pre, norm_mlp_post, w_in, sb_bias, conv_w, conv_b, gate_a_w, gate_a_b, gate_x_w, gate_x_b, lru_lambda, pool_w, pool_scale, w_branch, w_out, w_up, w_down):
    params = dict(norm_mix_pre=norm_mix_pre, norm_mix_post=norm_mix_post, norm_mlp_pre=norm_mlp_pre,
                  norm_mlp_post=norm_mlp_post, w_in=w_in, sb_bias=sb_bias, conv_w=conv_w, conv_b=conv_b,
                  gate_a_w=gate_a_w, gate_a_b=gate_a_b, gate_x_w=gate_x_w, gate_x_b=gate_x_b,
                  lru_lambda=lru_lambda, pool_w=pool_w, pool_scale=pool_scale, w_branch=w_branch,
                  w_out=w_out, w_up=w_up, w_down=w_down)
    b_p, seq, d = x_prompt.shape
    b_s, dec_seq, _ = x_sample.shape
    assert dec_seq == 1, "the sample group decodes one token per sequence"
    depth, n_phys, page, heads, head_dim = cache_k.shape
    assert head_dim == HEAD_DIM
    width = heads * head_dim
    n_conv, n_pool = CONV_W - 1, max(POOL_WINDOWS) - 1
    assert page_table.shape[1] * page > n_pool, "sample positions must have complete conv / pooling windows"
    pool_k = jnp.transpose(cache_k, (0, 1, 3, 4, 2)).reshape(depth * n_phys, width, page)
    pool_v = jnp.transpose(cache_v, (0, 1, 3, 4, 2)).reshape(depth * n_phys, width, page)

    def rows_to_heads(stack_t):
        return jnp.transpose(stack_t.reshape(depth, b_p, heads, head_dim, seq), (0, 1, 4, 2, 3))

    yp = x_prompt
    ys = x_sample.reshape(b_s, d)
    outs = {name: [] for name in ("ks", "vs", "hp", "hs", "cp", "cs", "pp", "ps")}
    kv_stacks = []
    for l in range(depth):
        lw = _layer_weights(l, width, params)
        qb, *kv_stacks, kvb, xr, xp = _prompt_in_proj(yp, lw["norm_mix_pre"], lw["w_q"], lw["w_kv_t"], lw["w_rp"],
                                                      ATT_TILE, l, depth, kv_stacks)
        oa = _prompt_attention(qb, kvb, lw["sb_bias"])
        ob, oc, h_last = _prompt_mixers(xr, xp, lw, SEQ_CHUNK)
        flat = lambda a: a.reshape(b_p * seq, -1)
        yp = _merge_mlp(flat(yp), flat(oa), flat(ob), flat(oc), lw, ROW_TILE).reshape(b_p, seq, d)
        outs["hp"].append(h_last.reshape(b_p, width))
        outs["cp"].append(xr[:, seq - n_conv:])
        outs["pp"].append(xp[:, seq - n_pool:])
        q, k, v, xr, xp = _sample_in_proj(ys, lw["norm_mix_pre"], lw["w_seq"])
        q_lanes = jnp.broadcast_to(q[:, :, None], (b_s, width, page))
        oa = _decode_attention(q_lanes, pool_k, pool_v, page_table, lw["sb_bias"], l * n_phys, PAGES_PER_STEP)
        ob, oc, h_new = _sample_mixers(xr, xp, jnp.swapaxes(state_conv[l], 0, 1), jnp.swapaxes(state_pool[l], 0, 1),
                                       state_h[l], lw)
        ys = _merge_mlp(ys, oa.reshape(b_s, width).astype(BF16), ob, oc, lw, b_s)
        outs["ks"].append(k.reshape(b_s, 1, heads, head_dim))
        outs["vs"].append(v.reshape(b_s, 1, heads, head_dim))
        outs["hs"].append(h_new)
        outs["cs"].append(jnp.concatenate([state_conv[l][:, 1:], xr[:, None, :]], axis=1))
        outs["ps"].append(jnp.concatenate([state_pool[l][:, 1:], xp[:, None, :]], axis=1))

    stack = lambda name: jnp.stack(outs[name])
    return (yp, ys.reshape(b_s, 1, d), rows_to_heads(kv_stacks[0]), rows_to_heads(kv_stacks[1]), stack("ks"), stack("vs"),
            stack("hp"), stack("hs"), stack("cp"), stack("cs"), stack("pp"), stack("ps"))
```

```python
import functools

import jax
import jax.numpy as jnp
from jax import lax
from jax.experimental import pallas as pl
from jax.experimental.pallas import tpu as pltpu

F32 = jnp.float32
BF16 = jnp.bfloat16

HEAD_DIM = 64
LRU_C = 8.0
CONV_W = 4
POOL_WINDOWS = (2, 4, 8, 16)
RMS_EPS = 1e-6

LANES = 128
SUBLANES = 8
HEADS_PER_BLOCK = LANES // HEAD_DIM
VMEM_LIMIT_BYTES = 56 * 1024 * 1024

ROW_TILE = 256
ATT_TILE = 256
ATT_ROWS = 32
SEQ_CHUNK = 256
PAGES_PER_STEP = 8
PAGE_RING_SLOTS = 3

_NT = (((1,), (1,)), ((), ()))


def _params(*sem):
    return pltpu.CompilerParams(dimension_semantics=sem, vmem_limit_bytes=VMEM_LIMIT_BYTES)


def _resident(shape):
    zeros = (0,) * len(shape)
    return pl.BlockSpec(shape, lambda *_: zeros, pipeline_mode=pl.Buffered(1))


def _rms(x, g):
    return x * lax.rsqrt(jnp.mean(x * x, axis=-1, keepdims=True) + RMS_EPS) * g


def _softplus(x):
    return jnp.maximum(x, 0.0) + jnp.log1p(jnp.exp(-jnp.abs(x)))


def _stick_logs(z):
    log_beta = jnp.minimum(z, 0.0) - jnp.log(1.0 + jnp.exp(-jnp.abs(z)))
    return log_beta, log_beta - z


def _split_bf16(x):
    hi = x.astype(BF16)
    lo = (x - hi.astype(F32)).astype(BF16)
    return hi, lo


def _prompt_in_proj_kernel(x_ref, g_ref, wq_ref, wkv_ref, wrp_ref, *refs, own):
    qb_ref, k_ref, v_ref, kvb_ref, xr_ref, xp_ref = refs[-6:]
    w = qb_ref.shape[-1]
    u = _rms(x_ref[0], g_ref[...]).astype(BF16)
    qb_ref[0] = (jnp.dot(u, wq_ref[...], preferred_element_type=F32) * (HEAD_DIM ** -0.5)).astype(BF16)
    kv = lax.dot_general(wkv_ref[...], u, _NT, preferred_element_type=F32)
    for layer in range(k_ref.shape[0]):
        k_ref[layer, 0] = kv[:w] if layer == own else jnp.zeros_like(kv[:w])
        v_ref[layer, 0] = kv[w:] if layer == own else jnp.zeros_like(kv[w:])
    kvb_ref[:, 0, 0] = kv.reshape(2, w, kv.shape[1]).astype(BF16)
    rp = jnp.dot(u, wrp_ref[...], preferred_element_type=F32)
    xr_ref[0] = rp[:, :w]
    xp_ref[0] = rp[:, w:]


def _prompt_in_proj(x, g, wq, wkv_t, wrp, tm, layer, depth, kv_stacks):
    bsz, t, d = x.shape
    width = wq.shape[1]
    rows = pl.BlockSpec((1, tm, width), lambda b, i: (b, i, 0))
    if kv_stacks:
        stack = pl.BlockSpec((1, 1, width, tm), lambda b, i: (layer, b, 0, i))
    else:
        stack = pl.BlockSpec((depth, 1, width, tm), lambda b, i: (0, b, 0, i))
    n_in = 5
    return pl.pallas_call(
        functools.partial(_prompt_in_proj_kernel, own=0 if kv_stacks else layer),
        grid=(bsz, t // tm),
        in_specs=[pl.BlockSpec((1, tm, d), lambda b, i: (b, i, 0)), _resident((1, d)),
                  _resident(wq.shape), _resident(wkv_t.shape), _resident(wrp.shape)]
                 + [pl.BlockSpec(memory_space=pl.ANY)] * len(kv_stacks),
        out_specs=[rows, stack, stack,
                   pl.BlockSpec((2, 1, 1, width, tm), lambda b, i: (0, b, i, 0, 0)),
                   rows, rows],
        out_shape=[jax.ShapeDtypeStruct((bsz, t, width), BF16),
                   jax.ShapeDtypeStruct((depth, bsz, width, t), F32),
                   jax.ShapeDtypeStruct((depth, bsz, width, t), F32),
                   jax.ShapeDtypeStruct((2, bsz, t // tm, width, tm), BF16),
                   jax.ShapeDtypeStruct((bsz, t, width), F32),
                   jax.ShapeDtypeStruct((bsz, t, width), F32)],
        input_output_aliases={n_in + n: 1 + n for n in range(len(kv_stacks))},
        compiler_params=_params("parallel", "parallel"),
        name="prompt_in_proj",
    )(x, g, wq, wkv_t, wrp, *kv_stacks)


def _sample_in_proj_kernel(x_ref, g_ref, w_ref, q_ref, k_ref, v_ref, xr_ref, xp_ref):
    w = k_ref.shape[-1]
    u = _rms(x_ref[...], g_ref[...]).astype(BF16)
    p = jnp.dot(u, w_ref[...], preferred_element_type=F32)
    q_ref[...] = p[:, :w] * (HEAD_DIM ** -0.5)
    k_ref[...] = p[:, w:2 * w]
    v_ref[...] = p[:, 2 * w:3 * w]
    xr_ref[...] = p[:, 3 * w:4 * w]
    xp_ref[...] = p[:, 4 * w:5 * w]


def _sample_in_proj(x, g, w):
    rows = x.shape[0]
    width = w.shape[1] // 5
    return pl.pallas_call(
        _sample_in_proj_kernel,
        out_shape=[jax.ShapeDtypeStruct((rows, width), F32)] * 5,
        compiler_params=pltpu.CompilerParams(vmem_limit_bytes=VMEM_LIMIT_BYTES),
        name="sample_in_proj",
    )(x, g, w)


def _attn_kernel(bias_ref, q_ref, kv_ref, o_ref, qh_ref, later_ref, s_ref, lb_ref, split_ref, in_ref, w_ref,
                 sum_ref, after_ref, acc_ref, *, tile, rows):
    pair = pl.program_id(1)
    i = pl.program_id(2)
    q = q_ref[0].astype(F32)
    first = lax.broadcasted_iota(jnp.int32, (tile, LANES), 1) < HEAD_DIM
    qh_ref[0] = jnp.where(first, q, 0.0).astype(BF16)
    qh_ref[1] = jnp.where(first, 0.0, q).astype(BF16)
    row = lax.broadcasted_iota(jnp.int32, (tile, tile), 0)
    col = lax.broadcasted_iota(jnp.int32, (tile, tile), 1)
    later = jnp.where(row > col, 1.0, 0.0).astype(BF16)
    later_ref[:tile] = later
    later_ref[tile:] = later
    acc_ref[...] = jnp.zeros_like(acc_ref)
    after_ref[...] = jnp.zeros_like(after_ref)
    chunks = [slice(r, r + rows) for r in range(0, tile, rows)]

    def logs_pass(h, diagonal):
        bias = bias_ref[HEADS_PER_BLOCK * pair + h]
        for c in chunks:
            log_beta, log_keep = _stick_logs(s_ref[h, c, :] + bias)
            if diagonal:
                causal = (lax.broadcasted_iota(jnp.int32, (rows, tile), 1)
                          < lax.broadcasted_iota(jnp.int32, (rows, tile), 0) + c.start)
                log_keep = jnp.where(causal, log_keep, 0.0)
            hi, lo = _split_bf16(log_keep)
            lb_ref[h, c, :] = log_beta
            split_ref[h, c, :tile] = hi
            split_ref[h, c, tile:] = lo
            sum_ref[h, c, :] = jnp.broadcast_to(jnp.sum(log_keep, axis=1, keepdims=True), (rows, LANES))

    def weights_pass(h, diagonal):
        for c in chunks:
            after = after_ref[h, c, :]
            w = jnp.exp(lb_ref[h, c, :] + in_ref[h, c, :] + jnp.concatenate([after] * (tile // LANES), axis=1))
            if diagonal:
                causal = (lax.broadcasted_iota(jnp.int32, (rows, tile), 1)
                          < lax.broadcasted_iota(jnp.int32, (rows, tile), 0) + c.start)
                w = jnp.where(causal, w, 0.0)
            w_ref[h, c, :] = w.astype(BF16)
            after_ref[h, c, :] = after + sum_ref[h, c, :]

    def scores(j, h):
        s_ref[h] = jnp.dot(qh_ref[h], kv_ref[0, 0, j], preferred_element_type=F32)

    def values(j):
        vt = kv_ref[1, 0, j]
        for h in range(HEADS_PER_BLOCK):
            acc_ref[h] += lax.dot_general(w_ref[h], vt, _NT, preferred_element_type=F32)

    def key_tile(j, diagonal):
        nxt = jnp.maximum(j - 1, 0)
        for h in range(HEADS_PER_BLOCK):
            logs_pass(h, diagonal)
            scores(nxt, h)
            in_ref[h] = jnp.dot(split_ref[h], later_ref[...], preferred_element_type=F32)
        for h in range(HEADS_PER_BLOCK):
            weights_pass(h, diagonal)

    for h in range(HEADS_PER_BLOCK):
        scores(i, h)
    key_tile(i, True)

    def body(t, carry):
        j = i - 1 - t
        values(j + 1)
        key_tile(j, False)
        return carry

    lax.fori_loop(0, i, body, 0)
    values(0)
    o_ref[0] = jnp.where(first, acc_ref[0], acc_ref[1]).astype(o_ref.dtype)


def _prompt_attention(qb, kvb, bias):
    bsz, t, width = qb.shape
    n_tiles, tile = kvb.shape[2], kvb.shape[4]
    per_head = lambda cols, dt: pltpu.VMEM((HEADS_PER_BLOCK, tile, cols), dt)
    return pl.pallas_call(
        functools.partial(_attn_kernel, tile=tile, rows=ATT_ROWS),
        grid=(bsz, width // LANES, n_tiles),
        in_specs=[
            pl.BlockSpec(memory_space=pltpu.SMEM),
            pl.BlockSpec((1, tile, LANES), lambda b, p, i: (b, i, p)),
            pl.BlockSpec((2, 1, n_tiles, LANES, tile), lambda b, p, i: (0, b, 0, p, 0)),
        ],
        out_specs=pl.BlockSpec((1, tile, LANES), lambda b, p, i: (b, i, p)),
        out_shape=jax.ShapeDtypeStruct((bsz, t, width), BF16),
        scratch_shapes=[per_head(LANES, BF16), pltpu.VMEM((2 * tile, tile), BF16), per_head(tile, F32),
                        per_head(tile, F32), per_head(2 * tile, BF16), per_head(tile, F32), per_head(tile, BF16),
                        per_head(LANES, F32), per_head(LANES, F32), per_head(LANES, F32)],
        compiler_params=_params("parallel", "parallel", "parallel"),
        name="prompt_attention",
    )(bias, qb, kvb)


def _decode_kernel(pt_ref, bias_ref, q_ref, pool_k, pool_v, o_ref, k_buf, v_buf, sems, acc_ref, after_ref, *,
                   n_pages, n_slots, layer_base):
    g = pl.program_id(1)
    n_steps = pl.num_programs(1)
    total = pl.num_programs(0) * n_steps
    step = pl.program_id(0) * n_steps + g
    _, _, width, page = k_buf.shape
    heads = width // HEAD_DIM

    def group_copies(b, grp, slot):
        copies = []
        for j in range(n_pages):
            src = layer_base + pt_ref[b, (n_steps - 1 - grp) * n_pages + j]
            copies.append(pltpu.make_async_copy(pool_k.at[src], k_buf.at[slot, j], sems.at[0, slot]))
            copies.append(pltpu.make_async_copy(pool_v.at[src], v_buf.at[slot, j], sems.at[1, slot]))
        return copies

    def start_group(m):
        for copy in group_copies(m // n_steps, m % n_steps, m % n_slots):
            copy.start()

    @pl.when(step == 0)
    def _():
        for m in range(n_slots - 1):
            start_group(m)

    ahead = step + (n_slots - 1)

    @pl.when(ahead < total)
    def _():
        start_group(ahead)

    slot = step % n_slots
    for copy in group_copies(pl.program_id(0), g, slot):
        copy.wait()

    @pl.when(g == 0)
    def _():
        acc_ref[...] = jnp.zeros_like(acc_ref)
        after_ref[...] = jnp.zeros_like(after_ref)

    hrow = lax.broadcasted_iota(jnp.int32, (heads, page), 0)
    bias = jnp.zeros((heads, page), F32)
    for h in range(heads):
        bias = jnp.where(hrow == h, bias_ref[h], bias)
    row = lax.broadcasted_iota(jnp.int32, (page, 2 * page), 0)
    col = lax.broadcasted_iota(jnp.int32, (page, 2 * page), 1)
    later = jnp.where((row > col) | (col >= page), 1.0, 0.0).astype(BF16)

    log_beta, his, los = [], [], []
    for j in range(n_pages):
        s = jnp.zeros((heads, page), F32)
        for h in range(heads):
            rows = slice(h * HEAD_DIM, (h + 1) * HEAD_DIM)
            dot = jnp.sum(k_buf[slot, j, rows, :] * q_ref[0, rows, :], axis=0, keepdims=True)
            s = jnp.where(hrow == h, dot, s)
        lb, lk = _stick_logs(s + bias)
        hi = lk.astype(BF16).astype(F32)
        log_beta.append(lb)
        his.append(hi)
        los.append(lk - hi)
    stacked = jnp.concatenate(his + los, axis=0).astype(BF16)
    sums = jnp.dot(stacked, later, preferred_element_type=F32)
    after = after_ref[...]
    weights = [None] * n_pages
    for j in reversed(range(n_pages)):
        both = sums[j * heads:(j + 1) * heads] + sums[(n_pages + j) * heads:(n_pages + j + 1) * heads]
        weights[j] = jnp.exp(log_beta[j] + both[:, :page] + after)
        after = after + both[:, page:]
    after_ref[...] = after
    for h in range(heads):
        rows = slice(h * HEAD_DIM, (h + 1) * HEAD_DIM)
        part = acc_ref[rows, :]
        for j in range(n_pages):
            part = part + v_buf[slot, j, rows, :] * weights[j][h:h + 1, :]
        acc_ref[rows, :] = part

    @pl.when(g == n_steps - 1)
    def _():
        o_ref[0] = jnp.sum(acc_ref[...], axis=1, keepdims=True)


def _decode_attention(q_lanes, pool_k, pool_v, page_table, bias, layer_base, n_pages, n_slots):
    bsz, width, page = q_lanes.shape
    n_steps = page_table.shape[1] // n_pages
    heads = width // HEAD_DIM
    assert n_slots - 1 <= bsz * n_steps
    grid_spec = pltpu.PrefetchScalarGridSpec(
        num_scalar_prefetch=1,
        grid=(bsz, n_steps),
        in_specs=[pl.BlockSpec(memory_space=pltpu.SMEM),
                  pl.BlockSpec((1, width, page), lambda b, g, pt: (b, 0, 0)),
                  pl.BlockSpec(memory_space=pl.ANY), pl.BlockSpec(memory_space=pl.ANY)],
        out_specs=pl.BlockSpec((1, width, 1), lambda b, g, pt: (b, 0, 0)),
        scratch_shapes=[pltpu.VMEM((n_slots, n_pages, width, page), F32),
                        pltpu.VMEM((n_slots, n_pages, width, page), F32),
                        pltpu.SemaphoreType.DMA((2, n_slots)),
                        pltpu.VMEM((width, page), F32), pltpu.VMEM((heads, page), F32)],
    )
    return pl.pallas_call(
        functools.partial(_decode_kernel, n_pages=n_pages, n_slots=n_slots, layer_base=layer_base),
        grid_spec=grid_spec,
        out_shape=jax.ShapeDtypeStruct((bsz, width, 1), F32),
        compiler_params=_params("arbitrary", "arbitrary"),
        name="decode_attention",
    )(page_table, bias, q_lanes, pool_k, pool_v)


def _shift_rows(x, s, fill):
    n, c = x.shape
    if s % SUBLANES == 0:
        return jnp.concatenate([jnp.full((s, c), fill, x.dtype), x[:n - s]], axis=0)
    rows = lax.broadcasted_iota(jnp.int32, x.shape, 0)
    return jnp.where(rows < s, fill, pltpu.roll(x, s, 0))


def _lru_gates(xc, wa_ref, ba_ref, wx_ref, bx_ref, neg_log_base):
    xb = xc.astype(BF16)
    r = jax.nn.sigmoid(jnp.dot(xb, wa_ref[...], preferred_element_type=F32) + ba_ref[...])
    gate = jax.nn.sigmoid(jnp.dot(xb, wx_ref[...], preferred_element_type=F32) + bx_ref[...])
    a = jnp.exp(-LRU_C * r * neg_log_base)
    return a, jnp.sqrt(1.0 - a * a), gate


def _pool_project(d_groups, pw_ref, ps_ref):
    y = [jnp.dot(d.astype(BF16), pw_ref[g], preferred_element_type=F32) for g, d in enumerate(d_groups)]
    return jnp.concatenate(y, axis=1) * ps_ref[...]


def _prompt_mixer_kernel(xr_ref, xp_ref, cw_ref, cb_ref, wa_ref, ba_ref, wx_ref, bx_ref, lam_ref,
                         pw_ref, ps_ref, ob_ref, oc_ref, hl_ref, *, chunk):
    t = xr_ref.shape[1]
    width = xr_ref.shape[2]
    group = width // len(POOL_WINDOWS)
    halo = 2 * SUBLANES
    neg_log_base = _softplus(-lam_ref[...])
    rows = lax.broadcasted_iota(jnp.int32, (chunk, 1), 0)

    def with_halo(ref, base, c):
        before = pl.multiple_of(jnp.maximum(base - halo, 0), SUBLANES)
        prev = jnp.where(c > 0, ref[0, pl.ds(before, halo), :], 0.0)
        return jnp.concatenate([prev, ref[0, pl.ds(base, chunk), :]], axis=0)

    def body(c, h_in):
        base = pl.multiple_of(c * chunk, chunk)
        pos = rows + base
        xs = with_halo(xr_ref, base, c)
        xc = cb_ref[...]
        for j in range(CONV_W):
            back = CONV_W - 1 - j
            tap = xs[halo:] if back == 0 else pltpu.roll(xs, back, 0)[halo:]
            xc = xc + tap * cw_ref[j:j + 1, :]
        a, mult, gate = _lru_gates(xc, wa_ref, ba_ref, wx_ref, bx_ref, neg_log_base)
        reset = pos == 0
        a = jnp.where(reset, 0.0, a)
        mult = jnp.where(reset, 1.0, mult)
        u = mult * gate * xc
        s = 1
        while s < chunk:
            a_prev = _shift_rows(a, s, 1.0)
            u_prev = _shift_rows(u, s, 0.0)
            u = u + a * u_prev
            a = a * a_prev
            s *= 2
        h = u + a * h_in
        ob_ref[0, pl.ds(base, chunk), :] = h.astype(ob_ref.dtype)
        ps = with_halo(xp_ref, base, c)
        diffs = []
        for g, win in enumerate(POOL_WINDOWS):
            x = ps[:, g * group:(g + 1) * group]
            total = x
            s = 1
            while s < win:
                total = total + _shift_rows(total, s, 0.0)
                s *= 2
            cnt = jnp.minimum(pos + 1, win).astype(F32)
            diffs.append(total[halo:] / cnt - x[halo:])
        oc_ref[0, pl.ds(base, chunk), :] = _pool_project(diffs, pw_ref, ps_ref).astype(oc_ref.dtype)
        return h[chunk - 1:chunk, :]

    h_last = lax.fori_loop(0, t // chunk, body, jnp.zeros((1, width), F32))
    hl_ref[0] = h_last


def _mixer_weights(lw):
    return (lw["conv_w"], lw["conv_b"], lw["gate_a_w"], lw["gate_a_b"], lw["gate_x_w"], lw["gate_x_b"],
            lw["lru_lambda"], lw["pool_w"], lw["pool_scale"])


def _prompt_mixers(xr, xp, lw, chunk):
    bsz, t, width = xr.shape
    seq = pl.BlockSpec((1, t, width), lambda b: (b, 0, 0))
    weights = _mixer_weights(lw)
    return pl.pallas_call(
        functools.partial(_prompt_mixer_kernel, chunk=chunk),
        grid=(bsz,),
        in_specs=[seq, seq] + [_resident(w.shape) for w in weights],
        out_specs=[seq, seq, pl.BlockSpec((1, 1, width), lambda b: (b, 0, 0))],
        out_shape=[jax.ShapeDtypeStruct((bsz, t, width), BF16), jax.ShapeDtypeStruct((bsz, t, width), BF16),
                   jax.ShapeDtypeStruct((bsz, 1, width), F32)],
        compiler_params=_params("parallel"),
        name="prompt_mixers",
    )(xr, xp, *weights)


def _sample_mixer_kernel(xr_ref, xp_ref, ch_ref, ph_ref, h0_ref, cw_ref, cb_ref, wa_ref, ba_ref, wx_ref, bx_ref,
                         lam_ref, pw_ref, ps_ref, ob_ref, oc_ref, h_ref):
    width = xr_ref.shape[1]
    group = width // len(POOL_WINDOWS)
    xc = cb_ref[...]
    for j in range(CONV_W - 1):
        xc = xc + ch_ref[j] * cw_ref[j:j + 1, :]
    xc = xc + xr_ref[...] * cw_ref[CONV_W - 1:CONV_W, :]
    a, mult, gate = _lru_gates(xc, wa_ref, ba_ref, wx_ref, bx_ref, _softplus(-lam_ref[...]))
    h = mult * gate * xc + a * h0_ref[...]
    h_ref[...] = h
    ob_ref[...] = h.astype(ob_ref.dtype)
    n_hist = ph_ref.shape[0]
    diffs = []
    for g, win in enumerate(POOL_WINDOWS):
        cols = slice(g * group, (g + 1) * group)
        x = xp_ref[:, cols]
        total = x
        for back in range(1, win):
            total = total + ph_ref[n_hist - back][:, cols]
        diffs.append(total / float(win) - x)
    oc_ref[...] = _pool_project(diffs, pw_ref, ps_ref).astype(oc_ref.dtype)


def _sample_mixers(xr, xp, conv_hist, pool_hist, h0, lw):
    bsz, width = xr.shape
    return pl.pallas_call(
        _sample_mixer_kernel,
        out_shape=[jax.ShapeDtypeStruct((bsz, width), BF16), jax.ShapeDtypeStruct((bsz, width), BF16),
                   jax.ShapeDtypeStruct((bsz, width), F32)],
        compiler_params=pltpu.CompilerParams(vmem_limit_bytes=VMEM_LIMIT_BYTES),
        name="sample_mixers",
    )(xr, xp, conv_hist, pool_hist, h0, *_mixer_weights(lw))


def _merge_mlp_kernel(x_ref, oa_ref, ob_ref, oc_ref, g_pre_ref, wg_ref, wb_ref, wo_ref, g_post_ref,
                      g_mlp_pre_ref, wu_ref, wd_ref, g_mlp_post_ref, y_ref):
    d = x_ref.shape[1]
    x = x_ref[...]
    u = _rms(x, g_pre_ref[...]).astype(BF16)
    mixed = None
    for n, branch in enumerate((oa_ref, ob_ref, oc_ref)):
        gate = jax.nn.sigmoid(jnp.dot(u, wg_ref[:, n * d:(n + 1) * d], preferred_element_type=F32))
        part = gate * jnp.dot(branch[...], wb_ref[n], preferred_element_type=F32)
        mixed = part if mixed is None else mixed + part
    mix_out = jnp.dot(mixed.astype(BF16), wo_ref[...], preferred_element_type=F32)
    x = x + _rms(mix_out, g_post_ref[...])
    u2 = _rms(x, g_mlp_pre_ref[...]).astype(BF16)
    hid = jnp.square(jnp.maximum(jnp.dot(u2, wu_ref[...], preferred_element_type=F32), 0.0))
    ffn = jnp.dot(hid.astype(BF16), wd_ref[...], preferred_element_type=F32)
    y_ref[...] = x + _rms(ffn, g_mlp_post_ref[...])


def _merge_mlp(x, oa, ob, oc, lw, tm):
    rows, d = x.shape
    width = oa.shape[1]
    weights = (lw["norm_mix_pre"], lw["w_gate"], lw["w_branch"], lw["w_out"], lw["norm_mix_post"],
               lw["norm_mlp_pre"], lw["w_up"], lw["w_down"], lw["norm_mlp_post"])
    row = pl.BlockSpec((tm, d), lambda i: (i, 0))
    branch = pl.BlockSpec((tm, width), lambda i: (i, 0))
    return pl.pallas_call(
        _merge_mlp_kernel,
        grid=(rows // tm,),
        in_specs=[row, branch, branch, branch] + [_resident(w.shape) for w in weights],
        out_specs=row,
        out_shape=jax.ShapeDtypeStruct((rows, d), F32),
        compiler_params=_params("parallel"),
        name="merge_mlp",
    )(x, oa, ob, oc, *weights)


def _block_diag(w):
    n, c, _ = w.shape
    eye = jnp.eye(n, dtype=w.dtype)
    return jnp.einsum("nij,nm->nimj", w, eye).reshape(n * c, n * c)


def _layer_weights(l, width, p):
    row = lambda a: a[l].reshape(1, -1)
    w_in = p["w_in"][l].astype(BF16)
    return {
        "norm_mix_pre": row(p["norm_mix_pre"]), "norm_mix_post": row(p["norm_mix_post"]),
        "norm_mlp_pre": row(p["norm_mlp_pre"]), "norm_mlp_post": row(p["norm_mlp_post"]),
        "w_seq": w_in[:, :5 * width], "w_q": w_in[:, :width], "w_kv_t": w_in[:, width:3 * width].T,
        "w_rp": w_in[:, 3 * width:5 * width], "w_gate": w_in[:, 5 * width:],
        "sb_bias": p["sb_bias"][l],
        "conv_w": p["conv_w"][l], "conv_b": row(p["conv_b"]),
        "gate_a_w": _block_diag(p["gate_a_w"][l]).astype(BF16), "gate_a_b": row(p["gate_a_b"]),
        "gate_x_w": _block_diag(p["gate_x_w"][l]).astype(BF16), "gate_x_b": row(p["gate_x_b"]),
        "lru_lambda": row(p["lru_lambda"]),
        "pool_w": p["pool_w"][l].astype(BF16), "pool_scale": row(p["pool_scale"]),
        "w_branch": p["w_branch"][l].astype(BF16), "w_out": p["w_out"][l].astype(BF16),
        "w_up": p["w_up"][l].astype(BF16), "w_down": p["w_down"][l].astype(BF16),
    }


def kernel(x_prompt, x_sample, cache_k, cache_v, page_table, state_h, state_conv, state_pool, norm_mix_pre, norm_mix_post, norm_mlp_pre, norm_mlp_post, w_in, sb_bias, conv_w, conv_b, gate_a_w, gate_a_b, gate_x_w, gate_x_b, lru_lambda, pool_w, pool_scale, w_branch, w_out, w_up, w_down):
    params = dict(norm_mix_pre=norm_mix_pre, norm_mix_post=norm_mix_post, norm_mlp_pre=norm_mlp_pre,
                  norm_mlp_post=norm_mlp_post, w_in=w_in, sb_bias=sb_bias, conv_w=conv_w, conv_b=conv_b,
                  gate_a_w=gate_a_w, gate_a_b=gate_a_b, gate_x_w=gate_x_w, gate_x_b=gate_x_b,
                  lru_lambda=lru_lambda, pool_w=pool_w, pool_scale=pool_scale, w_branch=w_branch,
                  w_out=w_out, w_up=w_up, w_down=w_down)
    b_p, seq, d = x_prompt.shape
    b_s, dec_seq, _ = x_sample.shape
    assert dec_seq == 1, "the sample group decodes one token per sequence"
    depth, n_phys, page, heads, head_dim = cache_k.shape
    assert head_dim == HEAD_DIM
    width = heads * head_dim
    n_conv, n_pool = CONV_W - 1, max(POOL_WINDOWS) - 1
    assert page_table.shape[1] * page > n_pool, "sample positions must have complete conv / pooling windows"
    pool_k = jnp.transpose(cache_k, (0, 1, 3, 4, 2)).reshape(depth * n_phys, width, page)
    pool_v = jnp.transpose(cache_v, (0, 1, 3, 4, 2)).reshape(depth * n_phys, width, page)

    def rows_to_heads(stack_t):
        return jnp.transpose(stack_t.reshape(depth, b_p, heads, head_dim, seq), (0, 1, 4, 2, 3))

    yp = x_prompt
    ys = x_sample.reshape(b_s, d)
    outs = {name: [] for name in ("ks", "vs", "hp", "hs", "cp", "cs", "pp", "ps")}
    kv_stacks = []
    for l in range(depth):
        lw = _layer_weights(l, width, params)
        qb, *kv_stacks, kvb, xr, xp = _prompt_in_proj(yp, lw["norm_mix_pre"], lw["w_q"], lw["w_kv_t"], lw["w_rp"],
                                                      ATT_TILE, l, depth, kv_stacks)
        oa = _prompt_attention(qb, kvb, lw["sb_bias"])
        ob, oc, h_last = _prompt_mixers(xr, xp, lw, SEQ_CHUNK)
        flat = lambda a: a.reshape(b_p * seq, -1)
        yp = _merge_mlp(flat(yp), flat(oa), flat(ob), flat(oc), lw, ROW_TILE).reshape(b_p, seq, d)
        outs["hp"].append(h_last.reshape(b_p, width))
        outs["cp"].append(xr[:, seq - n_conv:])
        outs["pp"].append(xp[:, seq - n_pool:])
        q, k, v, xr, xp = _sample_in_proj(ys, lw["norm_mix_pre"], lw["w_seq"])
        q_lanes = jnp.broadcast_to(q[:, :, None], (b_s, width, page))
        oa = _decode_attention(q_lanes, pool_k, pool_v, page_table, lw["sb_bias"], l * n_phys, PAGES_PER_STEP,
                               PAGE_RING_SLOTS)
        ob, oc, h_new = _sample_mixers(xr, xp, jnp.swapaxes(state_conv[l], 0, 1), jnp.swapaxes(state_pool[l], 0, 1),
                                       state_h[l], lw)
        ys = _merge_mlp(ys, oa.reshape(b_s, width).astype(BF16), ob, oc, lw, b_s)
        outs["ks"].append(k.reshape(b_s, 1, heads, head_dim))
        outs["vs"].append(v.reshape(b_s, 1, heads, head_dim))
        outs["hs"].append(h_new)
        outs["cs"].append(jnp.concatenate([state_conv[l][:, 1:], xr[:, None, :]], axis=1))
        outs["ps"].append(jnp.concatenate([state_pool[l][:, 1:], xp[:, None, :]], axis=1))

    stack = lambda name: jnp.stack(outs[name])
    return (yp, ys.reshape(b_s, 1, d), rows_to_heads(kv_stacks[0]), rows_to_heads(kv_stacks[1]), stack("ks"), stack("vs"),
            stack("hp"), stack("hs"), stack("cp"), stack("cs"), stack("pp"), stack("ps"))
```

```python
import functools

import jax
import jax.numpy as jnp
from jax import lax
from jax.experimental import pallas as pl
from jax.experimental.pallas import tpu as pltpu

F32 = jnp.float32
BF16 = jnp.bfloat16

HEAD_DIM = 64
LRU_C = 8.0
CONV_W = 4
POOL_WINDOWS = (2, 4, 8, 16)
RMS_EPS = 1e-6

LANES = 128
SUBLANES = 8
HEADS_PER_BLOCK = LANES // HEAD_DIM
VMEM_LIMIT_BYTES = 56 * 1024 * 1024

ROW_TILE = 256
ATT_TILE = 256
ATT_ROWS = 32
SEQ_CHUNK = 256
PAGES_PER_STEP = 8
PAGE_RING_SLOTS = 3

_NT = (((1,), (1,)), ((), ()))


def _params(*sem):
    return pltpu.CompilerParams(dimension_semantics=sem, vmem_limit_bytes=VMEM_LIMIT_BYTES)


def _resident(shape):
    zeros = (0,) * len(shape)
    return pl.BlockSpec(shape, lambda *_: zeros, pipeline_mode=pl.Buffered(1))


def _rms(x, g):
    return x * lax.rsqrt(jnp.mean(x * x, axis=-1, keepdims=True) + RMS_EPS) * g


def _softplus(x):
    return jnp.maximum(x, 0.0) + jnp.log1p(jnp.exp(-jnp.abs(x)))


def _stick_logs(z):
    log_beta = jnp.minimum(z, 0.0) - jnp.log(1.0 + jnp.exp(-jnp.abs(z)))
    return log_beta, log_beta - z


def _split_bf16(x):
    hi = x.astype(BF16)
    lo = (x - hi.astype(F32)).astype(BF16)
    return hi, lo


def _prompt_in_proj_kernel(x_ref, g_ref, wq_ref, wkv_ref, wrp_ref, *refs, own):
    qb_ref, k_ref, v_ref, kvb_ref, xr_ref, xp_ref = refs[-6:]
    w = qb_ref.shape[-1]
    u = _rms(x_ref[0], g_ref[...]).astype(BF16)
    qb_ref[0] = (jnp.dot(u, wq_ref[...], preferred_element_type=F32) * (HEAD_DIM ** -0.5)).astype(BF16)
    kv = lax.dot_general(wkv_ref[...], u, _NT, preferred_element_type=F32)
    for layer in range(k_ref.shape[0]):
        k_ref[layer, 0] = kv[:w] if layer == own else jnp.zeros_like(kv[:w])
        v_ref[layer, 0] = kv[w:] if layer == own else jnp.zeros_like(kv[w:])
    kvb_ref[:, 0, 0] = kv.reshape(2, w, kv.shape[1]).astype(BF16)
    rp = jnp.dot(u, wrp_ref[...], preferred_element_type=F32)
    xr_ref[0] = rp[:, :w]
    xp_ref[0] = rp[:, w:]


def _prompt_in_proj(x, g, wq, wkv_t, wrp, tm, layer, depth, kv_stacks):
    bsz, t, d = x.shape
    width = wq.shape[1]
    rows = pl.BlockSpec((1, tm, width), lambda b, i: (b, i, 0))
    if kv_stacks:
        stack = pl.BlockSpec((1, 1, width, tm), lambda b, i: (layer, b, 0, i))
    else:
        stack = pl.BlockSpec((depth, 1, width, tm), lambda b, i: (0, b, 0, i))
    n_in = 5
    return pl.pallas_call(
        functools.partial(_prompt_in_proj_kernel, own=0 if kv_stacks else layer),
        grid=(bsz, t // tm),
        in_specs=[pl.BlockSpec((1, tm, d), lambda b, i: (b, i, 0)), _resident((1, d)),
                  _resident(wq.shape), _resident(wkv_t.shape), _resident(wrp.shape)]
                 + [pl.BlockSpec(memory_space=pl.ANY)] * len(kv_stacks),
        out_specs=[rows, stack, stack,
                   pl.BlockSpec((2, 1, 1, width, tm), lambda b, i: (0, b, i, 0, 0)),
                   rows, rows],
        out_shape=[jax.ShapeDtypeStruct((bsz, t, width), BF16),
                   jax.ShapeDtypeStruct((depth, bsz, width, t), F32),
                   jax.ShapeDtypeStruct((depth, bsz, width, t), F32),
                   jax.ShapeDtypeStruct((2, bsz, t // tm, width, tm), BF16),
                   jax.ShapeDtypeStruct((bsz, t, width), F32),
                   jax.ShapeDtypeStruct((bsz, t, width), F32)],
        input_output_aliases={n_in + n: 1 + n for n in range(len(kv_stacks))},
        compiler_params=_params("parallel", "parallel"),
        name="prompt_in_proj",
    )(x, g, wq, wkv_t, wrp, *kv_stacks)


def _sample_in_proj_kernel(x_ref, g_ref, w_ref, q_ref, k_ref, v_ref, xr_ref, xp_ref):
    w = k_ref.shape[-1]
    u = _rms(x_ref[...], g_ref[...]).astype(BF16)
    p = jnp.dot(u, w_ref[...], preferred_element_type=F32)
    q_ref[...] = p[:, :w] * (HEAD_DIM ** -0.5)
    k_ref[...] = p[:, w:2 * w]
    v_ref[...] = p[:, 2 * w:3 * w]
    xr_ref[...] = p[:, 3 * w:4 * w]
    xp_ref[...] = p[:, 4 * w:5 * w]


def _sample_in_proj(x, g, w):
    rows = x.shape[0]
    width = w.shape[1] // 5
    return pl.pallas_call(
        _sample_in_proj_kernel,
        out_shape=[jax.ShapeDtypeStruct((rows, width), F32)] * 5,
        compiler_params=pltpu.CompilerParams(vmem_limit_bytes=VMEM_LIMIT_BYTES),
        name="sample_in_proj",
    )(x, g, w)


def _attn_kernel(bias_ref, q_ref, kv_ref, o_ref, qh_ref, later_ref, s_ref, lb_ref, split_ref, in_ref, w_ref,
                 sum_ref, after_ref, acc_ref, *, tile, rows):
    pair = pl.program_id(1)
    i = pl.program_id(2)
    q = q_ref[0].astype(F32)
    first = lax.broadcasted_iota(jnp.int32, (tile, LANES), 1) < HEAD_DIM
    qh_ref[0] = jnp.where(first, q, 0.0).astype(BF16)
    qh_ref[1] = jnp.where(first, 0.0, q).astype(BF16)
    row = lax.broadcasted_iota(jnp.int32, (tile, tile), 0)
    col = lax.broadcasted_iota(jnp.int32, (tile, tile), 1)
    later = jnp.where(row > col, 1.0, 0.0).astype(BF16)
    later_ref[:tile] = later
    later_ref[tile:] = later
    acc_ref[...] = jnp.zeros_like(acc_ref)
    after_ref[...] = jnp.zeros_like(after_ref)
    chunks = [slice(r, r + rows) for r in range(0, tile, rows)]

    def logs_pass(h, diagonal):
        bias = bias_ref[HEADS_PER_BLOCK * pair + h]
        for c in chunks:
            log_beta, log_keep = _stick_logs(s_ref[h, c, :] + bias)
            if diagonal:
                causal = (lax.broadcasted_iota(jnp.int32, (rows, tile), 1)
                          < lax.broadcasted_iota(jnp.int32, (rows, tile), 0) + c.start)
                log_keep = jnp.where(causal, log_keep, 0.0)
            hi, lo = _split_bf16(log_keep)
            lb_ref[h, c, :] = log_beta
            split_ref[h, c, :tile] = hi
            split_ref[h, c, tile:] = lo
            sum_ref[h, c, :] = jnp.broadcast_to(jnp.sum(log_keep, axis=1, keepdims=True), (rows, LANES))

    def weights_pass(h, diagonal):
        for c in chunks:
            after = after_ref[h, c, :]
            w = jnp.exp(lb_ref[h, c, :] + in_ref[h, c, :] + jnp.concatenate([after] * (tile // LANES), axis=1))
            if diagonal:
                causal = (lax.broadcasted_iota(jnp.int32, (rows, tile), 1)
                          < lax.broadcasted_iota(jnp.int32, (rows, tile), 0) + c.start)
                w = jnp.where(causal, w, 0.0)
            w_ref[h, c, :] = w.astype(BF16)
            after_ref[h, c, :] = after + sum_ref[h, c, :]

    def scores(j, h):
        s_ref[h] = jnp.dot(qh_ref[h], kv_ref[0, 0, j], preferred_element_type=F32)

    def values(j):
        vt = kv_ref[1, 0, j]
        for h in range(HEADS_PER_BLOCK):
            acc_ref[h] += lax.dot_general(w_ref[h], vt, _NT, preferred_element_type=F32)

    def key_tile(j, diagonal):
        nxt = jnp.maximum(j - 1, 0)
        for h in range(HEADS_PER_BLOCK):
            logs_pass(h, diagonal)
            scores(nxt, h)
            in_ref[h] = jnp.dot(split_ref[h], later_ref[...], preferred_element_type=F32)
        for h in range(HEADS_PER_BLOCK):
            weights_pass(h, diagonal)

    for h in range(HEADS_PER_BLOCK):
        scores(i, h)
    key_tile(i, True)

    def body(t, carry):
        j = i - 1 - t
        values(j + 1)
        key_tile(j, False)
        return carry

    lax.fori_loop(0, i, body, 0)
    values(0)
    o_ref[0] = jnp.where(first, acc_ref[0], acc_ref[1]).astype(o_ref.dtype)


def _prompt_attention(qb, kvb, bias):
    bsz, t, width = qb.shape
    n_tiles, tile = kvb.shape[2], kvb.shape[4]
    per_head = lambda cols, dt: pltpu.VMEM((HEADS_PER_BLOCK, tile, cols), dt)
    return pl.pallas_call(
        functools.partial(_attn_kernel, tile=tile, rows=ATT_ROWS),
        grid=(bsz, width // LANES, n_tiles),
        in_specs=[
            pl.BlockSpec(memory_space=pltpu.SMEM),
            pl.BlockSpec((1, tile, LANES), lambda b, p, i: (b, i, p)),
            pl.BlockSpec((2, 1, n_tiles, LANES, tile), lambda b, p, i: (0, b, 0, p, 0)),
        ],
        out_specs=pl.BlockSpec((1, tile, LANES), lambda b, p, i: (b, i, p)),
        out_shape=jax.ShapeDtypeStruct((bsz, t, width), BF16),
        scratch_shapes=[per_head(LANES, BF16), pltpu.VMEM((2 * tile, tile), BF16), per_head(tile, F32),
                        per_head(tile, F32), per_head(2 * tile, BF16), per_head(tile, F32), per_head(tile, BF16),
                        per_head(LANES, F32), per_head(LANES, F32), per_head(LANES, F32)],
        compiler_params=_params("parallel", "parallel", "parallel"),
        name="prompt_attention",
    )(bias, qb, kvb)


def _decode_bias(bias_ref, heads, page):
    hrow = lax.broadcasted_iota(jnp.int32, (heads, page), 0)
    bias = jnp.zeros((heads, page), F32)
    for h in range(heads):
        bias = jnp.where(hrow == h, bias_ref[h], bias)
    return hrow, bias


def _suffix_sums(x):
    n = x.shape[1]
    lane = lax.broadcasted_iota(jnp.int32, x.shape, 1)
    shift = 1
    while shift < n:
        ahead = pltpu.roll(x, n - shift, 1)
        x = x + jnp.where(lane < n - shift, ahead, 0.0)
        shift *= 2
    return x


def _decode_group(slot, q_ref, k_buf, v_buf, acc_ref, after_ref, hrow, bias):
    _, n_pages, width, page = k_buf.shape
    heads = width // HEAD_DIM
    scores = [jnp.zeros((heads, page), F32)] * n_pages
    for h in range(heads):
        rows = slice(h * HEAD_DIM, (h + 1) * HEAD_DIM)
        qh = q_ref[0, rows, :]
        for j in range(n_pages):
            dot = jnp.sum(k_buf[slot, j, rows, :] * qh, axis=0, keepdims=True)
            scores[j] = jnp.where(hrow == h, dot, scores[j])
    log_beta, log_keep = [], []
    for s in scores:
        lb, lk = _stick_logs(s + bias)
        log_beta.append(lb)
        log_keep.append(lk)
    keep = jnp.concatenate(log_keep, axis=0)
    from_here = _suffix_sums(keep)
    after = after_ref[...]
    weights = [None] * n_pages
    for j in reversed(range(n_pages)):
        rows = slice(j * heads, (j + 1) * heads)
        weights[j] = jnp.exp(log_beta[j] + (from_here[rows] - keep[rows]) + after)
        after = after + from_here[rows, :1]
    after_ref[...] = after
    for h in range(heads):
        rows = slice(h * HEAD_DIM, (h + 1) * HEAD_DIM)
        part = acc_ref[rows, :]
        for j in range(n_pages):
            part = part + v_buf[slot, j, rows, :] * weights[j][h:h + 1, :]
        acc_ref[rows, :] = part


def _shift_rows(x, s, fill):
    n, c = x.shape
    if s % SUBLANES == 0:
        return jnp.concatenate([jnp.full((s, c), fill, x.dtype), x[:n - s]], axis=0)
    rows = lax.broadcasted_iota(jnp.int32, x.shape, 0)
    return jnp.where(rows < s, fill, pltpu.roll(x, s, 0))


def _lru_gates(xc, wa_ref, ba_ref, wx_ref, bx_ref, neg_log_base):
    xb = xc.astype(BF16)
    r = jax.nn.sigmoid(jnp.dot(xb, wa_ref[...], preferred_element_type=F32) + ba_ref[...])
    gate = jax.nn.sigmoid(jnp.dot(xb, wx_ref[...], preferred_element_type=F32) + bx_ref[...])
    a = jnp.exp(-LRU_C * r * neg_log_base)
    return a, jnp.sqrt(1.0 - a * a), gate


def _pool_project(d_groups, pw_ref, ps_ref):
    y = [jnp.dot(d.astype(BF16), pw_ref[g], preferred_element_type=F32) for g, d in enumerate(d_groups)]
    return jnp.concatenate(y, axis=1) * ps_ref[...]


def _prompt_mixer_kernel(xr_ref, xp_ref, cw_ref, cb_ref, wa_ref, ba_ref, wx_ref, bx_ref, lam_ref,
                         pw_ref, ps_ref, ob_ref, oc_ref, hl_ref, *, chunk):
    t = xr_ref.shape[1]
    width = xr_ref.shape[2]
    group = width // len(POOL_WINDOWS)
    halo = 2 * SUBLANES
    neg_log_base = _softplus(-lam_ref[...])
    rows = lax.broadcasted_iota(jnp.int32, (chunk, 1), 0)

    def with_halo(ref, base, c):
        before = pl.multiple_of(jnp.maximum(base - halo, 0), SUBLANES)
        prev = jnp.where(c > 0, ref[0, pl.ds(before, halo), :], 0.0)
        return jnp.concatenate([prev, ref[0, pl.ds(base, chunk), :]], axis=0)

    def body(c, h_in):
        base = pl.multiple_of(c * chunk, chunk)
        pos = rows + base
        xs = with_halo(xr_ref, base, c)
        xc = cb_ref[...]
        for j in range(CONV_W):
            back = CONV_W - 1 - j
            tap = xs[halo:] if back == 0 else pltpu.roll(xs, back, 0)[halo:]
            xc = xc + tap * cw_ref[j:j + 1, :]
        a, mult, gate = _lru_gates(xc, wa_ref, ba_ref, wx_ref, bx_ref, neg_log_base)
        reset = pos == 0
        a = jnp.where(reset, 0.0, a)
        mult = jnp.where(reset, 1.0, mult)
        u = mult * gate * xc
        s = 1
        while s < chunk:
            a_prev = _shift_rows(a, s, 1.0)
            u_prev = _shift_rows(u, s, 0.0)
            u = u + a * u_prev
            a = a * a_prev
            s *= 2
        h = u + a * h_in
        ob_ref[0, pl.ds(base, chunk), :] = h.astype(ob_ref.dtype)
        ps = with_halo(xp_ref, base, c)
        diffs = []
        for g, win in enumerate(POOL_WINDOWS):
            x = ps[:, g * group:(g + 1) * group]
            total = x
            s = 1
            while s < win:
                total = total + _shift_rows(total, s, 0.0)
                s *= 2
            cnt = jnp.minimum(pos + 1, win).astype(F32)
            diffs.append(total[halo:] / cnt - x[halo:])
        oc_ref[0, pl.ds(base, chunk), :] = _pool_project(diffs, pw_ref, ps_ref).astype(oc_ref.dtype)
        return h[chunk - 1:chunk, :]

    h_last = lax.fori_loop(0, t // chunk, body, jnp.zeros((1, width), F32))
    hl_ref[0] = h_last


def _mixer_weights(lw):
    return (lw["conv_w"], lw["conv_b"], lw["gate_a_w"], lw["gate_a_b"], lw["gate_x_w"], lw["gate_x_b"],
            lw["lru_lambda"], lw["pool_w"], lw["pool_scale"])


def _prompt_mixers(xr, xp, lw, chunk):
    bsz, t, width = xr.shape
    seq = pl.BlockSpec((1, t, width), lambda b: (b, 0, 0))
    weights = _mixer_weights(lw)
    return pl.pallas_call(
        functools.partial(_prompt_mixer_kernel, chunk=chunk),
        grid=(bsz,),
        in_specs=[seq, seq] + [_resident(w.shape) for w in weights],
        out_specs=[seq, seq, pl.BlockSpec((1, 1, width), lambda b: (b, 0, 0))],
        out_shape=[jax.ShapeDtypeStruct((bsz, t, width), BF16), jax.ShapeDtypeStruct((bsz, t, width), BF16),
                   jax.ShapeDtypeStruct((bsz, 1, width), F32)],
        compiler_params=_params("parallel"),
        name="prompt_mixers",
    )(xr, xp, *weights)


def _sample_mixer_kernel(xr_ref, xp_ref, ch_ref, ph_ref, h0_ref, cw_ref, cb_ref, wa_ref, ba_ref, wx_ref, bx_ref,
                         lam_ref, pw_ref, ps_ref, ob_ref, oc_ref, h_ref):
    width = xr_ref.shape[1]
    group = width // len(POOL_WINDOWS)
    xc = cb_ref[...]
    for j in range(CONV_W - 1):
        xc = xc + ch_ref[j] * cw_ref[j:j + 1, :]
    xc = xc + xr_ref[...] * cw_ref[CONV_W - 1:CONV_W, :]
    a, mult, gate = _lru_gates(xc, wa_ref, ba_ref, wx_ref, bx_ref, _softplus(-lam_ref[...]))
    h = mult * gate * xc + a * h0_ref[...]
    h_ref[...] = h
    ob_ref[...] = h.astype(ob_ref.dtype)
    n_hist = ph_ref.shape[0]
    diffs = []
    for g, win in enumerate(POOL_WINDOWS):
        cols = slice(g * group, (g + 1) * group)
        x = xp_ref[:, cols]
        total = x
        for back in range(1, win):
            total = total + ph_ref[n_hist - back][:, cols]
        diffs.append(total / float(win) - x)
    oc_ref[...] = _pool_project(diffs, pw_ref, ps_ref).astype(oc_ref.dtype)


def _sample_mixers(xr, xp, conv_hist, pool_hist, h0, lw):
    bsz, width = xr.shape
    return pl.pallas_call(
        _sample_mixer_kernel,
        out_shape=[jax.ShapeDtypeStruct((bsz, width), BF16), jax.ShapeDtypeStruct((bsz, width), BF16),
                   jax.ShapeDtypeStruct((bsz, width), F32)],
        compiler_params=pltpu.CompilerParams(vmem_limit_bytes=VMEM_LIMIT_BYTES),
        name="sample_mixers",
    )(xr, xp, conv_hist, pool_hist, h0, *_mixer_weights(lw))


def _merge_mlp_kernel(x_ref, oa_ref, ob_ref, oc_ref, g_pre_ref, wg_ref, wb_ref, wo_ref, g_post_ref,
                      g_mlp_pre_ref, wu_ref, wd_ref, g_mlp_post_ref, y_ref):
    d = x_ref.shape[1]
    x = x_ref[...]
    u = _rms(x, g_pre_ref[...]).astype(BF16)
    mixed = None
    for n, branch in enumerate((oa_ref, ob_ref, oc_ref)):
        gate = jax.nn.sigmoid(jnp.dot(u, wg_ref[:, n * d:(n + 1) * d], preferred_element_type=F32))
        part = gate * jnp.dot(branch[...], wb_ref[n], preferred_element_type=F32)
        mixed = part if mixed is None else mixed + part
    mix_out = jnp.dot(mixed.astype(BF16), wo_ref[...], preferred_element_type=F32)
    x = x + _rms(mix_out, g_post_ref[...])
    u2 = _rms(x, g_mlp_pre_ref[...]).astype(BF16)
    hid = jnp.square(jnp.maximum(jnp.dot(u2, wu_ref[...], preferred_element_type=F32), 0.0))
    ffn = jnp.dot(hid.astype(BF16), wd_ref[...], preferred_element_type=F32)
    y_ref[...] = x + _rms(ffn, g_mlp_post_ref[...])


def _merge_mlp(x, oa, ob, oc, lw, tm):
    rows, d = x.shape
    width = oa.shape[1]
    weights = (lw["norm_mix_pre"], lw["w_gate"], lw["w_branch"], lw["w_out"], lw["norm_mix_post"],
               lw["norm_mlp_pre"], lw["w_up"], lw["w_down"], lw["norm_mlp_post"])
    row = pl.BlockSpec((tm, d), lambda i: (i, 0))
    branch = pl.BlockSpec((tm, width), lambda i: (i, 0))
    return pl.pallas_call(
        _merge_mlp_kernel,
        grid=(rows // tm,),
        in_specs=[row, branch, branch, branch] + [_resident(w.shape) for w in weights],
        out_specs=row,
        out_shape=jax.ShapeDtypeStruct((rows, d), F32),
        compiler_params=_params("parallel"),
        name="merge_mlp",
    )(x, oa, ob, oc, *weights)


def _merge_mlp_decode_kernel(pt_ref, bias_ref, q_ref, pool_k, pool_v, x_ref, oa_ref, ob_ref, oc_ref, g_pre_ref,
                             wg_ref, wb_ref, wo_ref, g_post_ref, g_mlp_pre_ref, wu_ref, wd_ref, g_mlp_post_ref,
                             y_ref, o_ref, k_buf, v_buf, sems, acc_ref, after_ref, *, n_slots, layer_base, n_regions):
    step = pl.program_id(0)
    d = x_ref.shape[1]
    _, n_pages, width, page = k_buf.shape
    heads = width // HEAD_DIM
    groups_per_seq = pt_ref.shape[1] // n_pages
    total = pl.num_programs(0) * n_regions
    n_branch = wb_ref.shape[0]
    n_ff = n_regions - n_branch - 1
    ff = wu_ref.shape[1] // n_ff
    hrow, bias = _decode_bias(bias_ref, heads, page)

    def group_copies(n, slot):
        b = n // groups_per_seq
        grp = n % groups_per_seq
        copies = []
        for j in range(n_pages):
            src = layer_base + pt_ref[b, (groups_per_seq - 1 - grp) * n_pages + j]
            copies.append(pltpu.make_async_copy(pool_k.at[src], k_buf.at[slot, j], sems.at[0, slot]))
            copies.append(pltpu.make_async_copy(pool_v.at[src], v_buf.at[slot, j], sems.at[1, slot]))
        return copies

    def start_group(n):
        for copy in group_copies(n, n % n_slots):
            copy.start()

    def begin_piece(r):
        n = step * n_regions + r
        ahead = n + (n_slots - 1)

        @pl.when(ahead < total)
        def _():
            start_group(ahead)

        slot = n % n_slots
        for copy in group_copies(n, slot):
            copy.wait()

        @pl.when(n % groups_per_seq == 0)
        def _():
            acc_ref[...] = jnp.zeros_like(acc_ref)
            after_ref[...] = jnp.zeros_like(after_ref)

        return slot

    def end_piece(r, slot):
        _decode_group(slot, q_ref, k_buf, v_buf, acc_ref, after_ref, hrow, bias)

        @pl.when((step * n_regions + r) % groups_per_seq == groups_per_seq - 1)
        def _():
            o_ref[0] = jnp.sum(acc_ref[...], axis=1, keepdims=True)

    @pl.when(step == 0)
    def _():
        for n in range(n_slots - 1):
            start_group(n)

    x = x_ref[...]
    u = _rms(x, g_pre_ref[...]).astype(BF16)
    mixed = None
    for n, branch in enumerate((oa_ref, ob_ref, oc_ref)):
        slot = begin_piece(n)
        gate = jax.nn.sigmoid(jnp.dot(u, wg_ref[:, n * d:(n + 1) * d], preferred_element_type=F32))
        part = gate * jnp.dot(branch[...], wb_ref[n], preferred_element_type=F32)
        mixed = part if mixed is None else mixed + part
        end_piece(n, slot)
    slot = begin_piece(n_branch)
    mix_out = jnp.dot(mixed.astype(BF16), wo_ref[...], preferred_element_type=F32)
    x = x + _rms(mix_out, g_post_ref[...])
    u2 = _rms(x, g_mlp_pre_ref[...]).astype(BF16)
    up = lambda c: jnp.square(jnp.maximum(
        jnp.dot(u2, wu_ref[:, c * ff:(c + 1) * ff], preferred_element_type=F32), 0.0)).astype(BF16)
    hid = up(0)
    end_piece(n_branch, slot)
    ffn = None
    for c in range(n_ff):
        slot = begin_piece(n_branch + 1 + c)
        down = jnp.dot(hid, wd_ref[c * ff:(c + 1) * ff, :], preferred_element_type=F32)
        ffn = down if ffn is None else ffn + down
        if c + 1 < n_ff:
            hid = up(c + 1)
        end_piece(n_branch + 1 + c, slot)
    y_ref[...] = x + _rms(ffn, g_mlp_post_ref[...])


def _merge_mlp_decode(x, oa, ob, oc, lw, tm, q_lanes, pool_k, pool_v, page_table, layer_base, n_pages, n_slots):
    rows, d = x.shape
    bsz, width, page = q_lanes.shape
    heads = width // HEAD_DIM
    weights = (lw["norm_mix_pre"], lw["w_gate"], lw["w_branch"], lw["w_out"], lw["norm_mix_post"],
               lw["norm_mlp_pre"], lw["w_up"], lw["w_down"], lw["norm_mlp_post"])
    n_steps = rows // tm
    groups_per_seq = page_table.shape[1] // n_pages
    n_regions = bsz * groups_per_seq // n_steps
    assert n_regions * n_steps == bsz * groups_per_seq and groups_per_seq % n_regions == 0
    assert n_regions > lw["w_branch"].shape[0] + 1 and n_slots - 1 <= n_regions
    steps_per_seq = groups_per_seq // n_regions
    row = pl.BlockSpec((tm, d), lambda i, pt: (i, 0))
    branch = pl.BlockSpec((tm, oa.shape[1]), lambda i, pt: (i, 0))
    seq = lambda last: pl.BlockSpec((1, width, last), lambda i, pt: (i // steps_per_seq, 0, 0))
    resident = lambda w: pl.BlockSpec(w.shape, lambda i, pt: (0,) * w.ndim, pipeline_mode=pl.Buffered(1))
    grid_spec = pltpu.PrefetchScalarGridSpec(
        num_scalar_prefetch=1,
        grid=(n_steps,),
        in_specs=[pl.BlockSpec(memory_space=pltpu.SMEM), seq(page),
                  pl.BlockSpec(memory_space=pl.ANY), pl.BlockSpec(memory_space=pl.ANY),
                  row, branch, branch, branch] + [resident(w) for w in weights],
        out_specs=[row, seq(1)],
        scratch_shapes=[pltpu.VMEM((n_slots, n_pages, width, page), F32),
                        pltpu.VMEM((n_slots, n_pages, width, page), F32),
                        pltpu.SemaphoreType.DMA((2, n_slots)),
                        pltpu.VMEM((width, page), F32), pltpu.VMEM((heads, page), F32)],
    )
    return pl.pallas_call(
        functools.partial(_merge_mlp_decode_kernel, n_slots=n_slots, layer_base=layer_base, n_regions=n_regions),
        grid_spec=grid_spec,
        out_shape=[jax.ShapeDtypeStruct((rows, d), F32), jax.ShapeDtypeStruct((bsz, width, 1), F32)],
        compiler_params=_params("arbitrary"),
        name="merge_mlp_decode",
    )(page_table, lw["sb_bias"], q_lanes, pool_k, pool_v, x, oa, ob, oc, *weights)


def _block_diag(w):
    n, c, _ = w.shape
    eye = jnp.eye(n, dtype=w.dtype)
    return jnp.einsum("nij,nm->nimj", w, eye).reshape(n * c, n * c)


def _layer_weights(l, width, p):
    row = lambda a: a[l].reshape(1, -1)
    w_in = p["w_in"][l].astype(BF16)
    return {
        "norm_mix_pre": row(p["norm_mix_pre"]), "norm_mix_post": row(p["norm_mix_post"]),
        "norm_mlp_pre": row(p["norm_mlp_pre"]), "norm_mlp_post": row(p["norm_mlp_post"]),
        "w_seq": w_in[:, :5 * width], "w_q": w_in[:, :width], "w_kv_t": w_in[:, width:3 * width].T,
        "w_rp": w_in[:, 3 * width:5 * width], "w_gate": w_in[:, 5 * width:],
        "sb_bias": p["sb_bias"][l],
        "conv_w": p["conv_w"][l], "conv_b": row(p["conv_b"]),
        "gate_a_w": _block_diag(p["gate_a_w"][l]).astype(BF16), "gate_a_b": row(p["gate_a_b"]),
        "gate_x_w": _block_diag(p["gate_x_w"][l]).astype(BF16), "gate_x_b": row(p["gate_x_b"]),
        "lru_lambda": row(p["lru_lambda"]),
        "pool_w": p["pool_w"][l].astype(BF16), "pool_scale": row(p["pool_scale"]),
        "w_branch": p["w_branch"][l].astype(BF16), "w_out": p["w_out"][l].astype(BF16),
        "w_up": p["w_up"][l].astype(BF16), "w_down": p["w_down"][l].astype(BF16),
    }


def kernel(x_prompt, x_sample, cache_k, cache_v, page_table, state_h, state_conv, state_pool, norm_mix_pre, norm_mix_post, norm_mlp_pre, norm_mlp_post, w_in, sb_bias, conv_w, conv_b, gate_a_w, gate_a_b, gate_x_w, gate_x_b, lru_lambda, pool_w, pool_scale, w_branch, w_out, w_up, w_down):
    params = dict(norm_mix_pre=norm_mix_pre, norm_mix_post=norm_mix_post, norm_mlp_pre=norm_mlp_pre,
                  norm_mlp_post=norm_mlp_post, w_in=w_in, sb_bias=sb_bias, conv_w=conv_w, conv_b=conv_b,
                  gate_a_w=gate_a_w, gate_a_b=gate_a_b, gate_x_w=gate_x_w, gate_x_b=gate_x_b,
                  lru_lambda=lru_lambda, pool_w=pool_w, pool_scale=pool_scale, w_branch=w_branch,
                  w_out=w_out, w_up=w_up, w_down=w_down)
    b_p, seq, d = x_prompt.shape
    b_s, dec_seq, _ = x_sample.shape
    assert dec_seq == 1, "the sample group decodes one token per sequence"
    depth, n_phys, page, heads, head_dim = cache_k.shape
    assert head_dim == HEAD_DIM
    width = heads * head_dim
    n_conv, n_pool = CONV_W - 1, max(POOL_WINDOWS) - 1
    assert page_table.shape[1] * page > n_pool, "sample positions must have complete conv / pooling windows"
    pool_k = jnp.transpose(cache_k, (0, 1, 3, 4, 2)).reshape(depth * n_phys, width, page)
    pool_v = jnp.transpose(cache_v, (0, 1, 3, 4, 2)).reshape(depth * n_phys, width, page)

    def rows_to_heads(stack_t):
        return jnp.transpose(stack_t.reshape(depth, b_p, heads, head_dim, seq), (0, 1, 4, 2, 3))

    yp = x_prompt
    ys = x_sample.reshape(b_s, d)
    outs = {name: [] for name in ("ks", "vs", "hp", "hs", "cp", "cs", "pp", "ps")}
    kv_stacks = []
    for l in range(depth):
        lw = _layer_weights(l, width, params)
        qb, *kv_stacks, kvb, xr, xp = _prompt_in_proj(yp, lw["norm_mix_pre"], lw["w_q"], lw["w_kv_t"], lw["w_rp"],
                                                      ATT_TILE, l, depth, kv_stacks)
        oa = _prompt_attention(qb, kvb, lw["sb_bias"])
        ob, oc, h_last = _prompt_mixers(xr, xp, lw, SEQ_CHUNK)
        q, k, v, xr_s, xp_s = _sample_in_proj(ys, lw["norm_mix_pre"], lw["w_seq"])
        q_lanes = jnp.broadcast_to(q[:, :, None], (b_s, width, page))
        flat = lambda a: a.reshape(b_p * seq, -1)
        yp, oa = _merge_mlp_decode(flat(yp), flat(oa), flat(ob), flat(oc), lw, ROW_TILE, q_lanes, pool_k, pool_v,
                                   page_table, l * n_phys, PAGES_PER_STEP, PAGE_RING_SLOTS)
        yp = yp.reshape(b_p, seq, d)
        outs["hp"].append(h_last.reshape(b_p, width))
        outs["cp"].append(xr[:, seq - n_conv:])
        outs["pp"].append(xp[:, seq - n_pool:])
        xr, xp = xr_s, xp_s
        ob, oc, h_new = _sample_mixers(xr, xp, jnp.swapaxes(state_conv[l], 0, 1), jnp.swapaxes(state_pool[l], 0, 1),
                                       state_h[l], lw)
        ys = _merge_mlp(ys, oa.reshape(b_s, width).astype(BF16), ob, oc, lw, b_s)
        outs["ks"].append(k.reshape(b_s, 1, heads, head_dim))
        outs["vs"].append(v.reshape(b_s, 1, heads, head_dim))
        outs["hs"].append(h_new)
        outs["cs"].append(jnp.concatenate([state_conv[l][:, 1:], xr[:, None, :]], axis=1))
        outs["ps"].append(jnp.concatenate([state_pool[l][:, 1:], xp[:, None, :]], axis=1))

    stack = lambda name: jnp.stack(outs[name])
    return (yp, ys.reshape(b_s, 1, d), rows_to_heads(kv_stacks[0]), rows_to_heads(kv_stacks[1]), stack("ks"), stack("vs"),
            stack("hp"), stack("hs"), stack("cp"), stack("cs"), stack("pp"), stack("ps"))
```

```python
import functools

import jax
import jax.numpy as jnp
from jax import lax
from jax.experimental import pallas as pl
from jax.experimental.pallas import tpu as pltpu

F32 = jnp.float32
BF16 = jnp.bfloat16

HEAD_DIM = 64
LRU_C = 8.0
CONV_W = 4
POOL_WINDOWS = (2, 4, 8, 16)
RMS_EPS = 1e-6

LANES = 128
SUBLANES = 8
HEADS_PER_BLOCK = LANES // HEAD_DIM
VMEM_LIMIT_BYTES = 56 * 1024 * 1024

ROW_TILE = 256
ATT_TILE = 256
ATT_ROWS = 32
SEQ_CHUNK = 256
PAGES_PER_STEP = 8
PAGE_RING_SLOTS = 3

_NT = (((1,), (1,)), ((), ()))


def _params(*sem):
    return pltpu.CompilerParams(dimension_semantics=sem, vmem_limit_bytes=VMEM_LIMIT_BYTES)


def _resident(shape):
    zeros = (0,) * len(shape)
    return pl.BlockSpec(shape, lambda *_: zeros, pipeline_mode=pl.Buffered(1))


def _rms(x, g):
    return x * lax.rsqrt(jnp.mean(x * x, axis=-1, keepdims=True) + RMS_EPS) * g


def _softplus(x):
    return jnp.maximum(x, 0.0) + jnp.log1p(jnp.exp(-jnp.abs(x)))


def _stick_logs(z):
    log_beta = jnp.minimum(z, 0.0) - jnp.log(1.0 + jnp.exp(-jnp.abs(z)))
    return log_beta, log_beta - z


def _split_bf16(x):
    hi = x.astype(BF16)
    lo = (x - hi.astype(F32)).astype(BF16)
    return hi, lo


def _prompt_in_proj_kernel(x_ref, g_ref, wq_ref, wkv_ref, wrp_ref, *refs, own):
    qb_ref, k_ref, v_ref, kvb_ref, xr_ref, xp_ref = refs[-6:]
    w = qb_ref.shape[-1]
    u = _rms(x_ref[0], g_ref[...]).astype(BF16)
    qb_ref[0] = (jnp.dot(u, wq_ref[...], preferred_element_type=F32) * (HEAD_DIM ** -0.5)).astype(BF16)
    kv = lax.dot_general(wkv_ref[...], u, _NT, preferred_element_type=F32)
    for layer in range(k_ref.shape[0]):
        k_ref[layer, 0] = kv[:w] if layer == own else jnp.zeros_like(kv[:w])
        v_ref[layer, 0] = kv[w:] if layer == own else jnp.zeros_like(kv[w:])
    kvb_ref[:, 0, 0] = kv.reshape(2, w, kv.shape[1]).astype(BF16)
    rp = jnp.dot(u, wrp_ref[...], preferred_element_type=F32)
    xr_ref[0] = rp[:, :w]
    xp_ref[0] = rp[:, w:]


def _prompt_in_proj(x, g, wq, wkv_t, wrp, tm, layer, depth, kv_stacks):
    bsz, t, d = x.shape
    width = wq.shape[1]
    rows = pl.BlockSpec((1, tm, width), lambda b, i: (b, i, 0))
    if kv_stacks:
        stack = pl.BlockSpec((1, 1, width, tm), lambda b, i: (layer, b, 0, i))
    else:
        stack = pl.BlockSpec((depth, 1, width, tm), lambda b, i: (0, b, 0, i))
    n_in = 5
    return pl.pallas_call(
        functools.partial(_prompt_in_proj_kernel, own=0 if kv_stacks else layer),
        grid=(bsz, t // tm),
        in_specs=[pl.BlockSpec((1, tm, d), lambda b, i: (b, i, 0)), _resident((1, d)),
                  _resident(wq.shape), _resident(wkv_t.shape), _resident(wrp.shape)]
                 + [pl.BlockSpec(memory_space=pl.ANY)] * len(kv_stacks),
        out_specs=[rows, stack, stack,
                   pl.BlockSpec((2, 1, 1, width, tm), lambda b, i: (0, b, i, 0, 0)),
                   rows, rows],
        out_shape=[jax.ShapeDtypeStruct((bsz, t, width), BF16),
                   jax.ShapeDtypeStruct((depth, bsz, width, t), F32),
                   jax.ShapeDtypeStruct((depth, bsz, width, t), F32),
                   jax.ShapeDtypeStruct((2, bsz, t // tm, width, tm), BF16),
                   jax.ShapeDtypeStruct((bsz, t, width), F32),
                   jax.ShapeDtypeStruct((bsz, t, width), F32)],
        input_output_aliases={n_in + n: 1 + n for n in range(len(kv_stacks))},
        compiler_params=_params("parallel", "parallel"),
        name="prompt_in_proj",
    )(x, g, wq, wkv_t, wrp, *kv_stacks)


def _sample_in_proj_kernel(x_ref, g_ref, w_ref, q_ref, k_ref, v_ref, xr_ref, xp_ref):
    w = k_ref.shape[-1]
    u = _rms(x_ref[...], g_ref[...]).astype(BF16)
    p = jnp.dot(u, w_ref[...], preferred_element_type=F32)
    q_ref[...] = p[:, :w] * (HEAD_DIM ** -0.5)
    k_ref[...] = p[:, w:2 * w]
    v_ref[...] = p[:, 2 * w:3 * w]
    xr_ref[...] = p[:, 3 * w:4 * w]
    xp_ref[...] = p[:, 4 * w:5 * w]


def _sample_in_proj(x, g, w):
    rows = x.shape[0]
    width = w.shape[1] // 5
    return pl.pallas_call(
        _sample_in_proj_kernel,
        out_shape=[jax.ShapeDtypeStruct((rows, width), F32)] * 5,
        compiler_params=pltpu.CompilerParams(vmem_limit_bytes=VMEM_LIMIT_BYTES),
        name="sample_in_proj",
    )(x, g, w)


def _attn_kernel(bias_ref, q_ref, kv_ref, o_ref, qh_ref, later_ref, s_ref, lb_ref, split_ref, in_ref, w_ref,
                 sum_ref, after_ref, acc_ref, *, tile, rows):
    pair = pl.program_id(1)
    i = pl.program_id(2)
    q = q_ref[0].astype(F32)
    first = lax.broadcasted_iota(jnp.int32, (tile, LANES), 1) < HEAD_DIM
    qh_ref[0] = jnp.where(first, q, 0.0).astype(BF16)
    qh_ref[1] = jnp.where(first, 0.0, q).astype(BF16)
    row = lax.broadcasted_iota(jnp.int32, (tile, tile), 0)
    col = lax.broadcasted_iota(jnp.int32, (tile, tile), 1)
    later = jnp.where(row > col, 1.0, 0.0).astype(BF16)
    later_ref[:tile] = later
    later_ref[tile:] = later
    acc_ref[...] = jnp.zeros_like(acc_ref)
    after_ref[...] = jnp.zeros_like(after_ref)
    chunks = [slice(r, r + rows) for r in range(0, tile, rows)]

    def logs_pass(h, diagonal):
        bias = bias_ref[HEADS_PER_BLOCK * pair + h]
        for c in chunks:
            log_beta, log_keep = _stick_logs(s_ref[h, c, :] + bias)
            if diagonal:
                causal = (lax.broadcasted_iota(jnp.int32, (rows, tile), 1)
                          < lax.broadcasted_iota(jnp.int32, (rows, tile), 0) + c.start)
                log_keep = jnp.where(causal, log_keep, 0.0)
            hi, lo = _split_bf16(log_keep)
            lb_ref[h, c, :] = log_beta
            split_ref[h, c, :tile] = hi
            split_ref[h, c, tile:] = lo
            sum_ref[h, c, :] = jnp.broadcast_to(jnp.sum(log_keep, axis=1, keepdims=True), (rows, LANES))

    def weights_pass(h, diagonal):
        for c in chunks:
            after = after_ref[h, c, :]
            w = jnp.exp(lb_ref[h, c, :] + in_ref[h, c, :] + jnp.concatenate([after] * (tile // LANES), axis=1))
            if diagonal:
                causal = (lax.broadcasted_iota(jnp.int32, (rows, tile), 1)
                          < lax.broadcasted_iota(jnp.int32, (rows, tile), 0) + c.start)
                w = jnp.where(causal, w, 0.0)
            w_ref[h, c, :] = w.astype(BF16)
            after_ref[h, c, :] = after + sum_ref[h, c, :]

    def scores(j, h):
        s_ref[h] = jnp.dot(qh_ref[h], kv_ref[0, 0, j], preferred_element_type=F32)

    def values(j):
        vt = kv_ref[1, 0, j]
        for h in range(HEADS_PER_BLOCK):
            acc_ref[h] += lax.dot_general(w_ref[h], vt, _NT, preferred_element_type=F32)

    def key_tile(j, diagonal):
        nxt = jnp.maximum(j - 1, 0)
        for h in range(HEADS_PER_BLOCK):
            logs_pass(h, diagonal)
            scores(nxt, h)
            in_ref[h] = jnp.dot(split_ref[h], later_ref[...], preferred_element_type=F32)
        for h in range(HEADS_PER_BLOCK):
            weights_pass(h, diagonal)

    for h in range(HEADS_PER_BLOCK):
        scores(i, h)
    key_tile(i, True)

    def body(t, carry):
        j = i - 1 - t
        values(j + 1)
        key_tile(j, False)
        return carry

    lax.fori_loop(0, i, body, 0)
    values(0)
    o_ref[0] = jnp.where(first, acc_ref[0], acc_ref[1]).astype(o_ref.dtype)


def _prompt_attention(qb, kvb, bias):
    bsz, t, width = qb.shape
    n_tiles, tile = kvb.shape[2], kvb.shape[4]
    per_head = lambda cols, dt: pltpu.VMEM((HEADS_PER_BLOCK, tile, cols), dt)
    return pl.pallas_call(
        functools.partial(_attn_kernel, tile=tile, rows=ATT_ROWS),
        grid=(bsz, width // LANES, n_tiles),
        in_specs=[
            pl.BlockSpec(memory_space=pltpu.SMEM),
            pl.BlockSpec((1, tile, LANES), lambda b, p, i: (b, i, p)),
            pl.BlockSpec((2, 1, n_tiles, LANES, tile), lambda b, p, i: (0, b, 0, p, 0)),
        ],
        out_specs=pl.BlockSpec((1, tile, LANES), lambda b, p, i: (b, i, p)),
        out_shape=jax.ShapeDtypeStruct((bsz, t, width), BF16),
        scratch_shapes=[per_head(LANES, BF16), pltpu.VMEM((2 * tile, tile), BF16), per_head(tile, F32),
                        per_head(tile, F32), per_head(2 * tile, BF16), per_head(tile, F32), per_head(tile, BF16),
                        per_head(LANES, F32), per_head(LANES, F32), per_head(LANES, F32)],
        compiler_params=_params("parallel", "parallel", "parallel"),
        name="prompt_attention",
    )(bias, qb, kvb)


def _decode_bias(bias_ref, heads, page):
    hrow = lax.broadcasted_iota(jnp.int32, (heads, page), 0)
    bias = jnp.zeros((heads, page), F32)
    for h in range(heads):
        bias = jnp.where(hrow == h, bias_ref[h], bias)
    return hrow, bias


def _suffix_sums(x):
    n = x.shape[1]
    lane = lax.broadcasted_iota(jnp.int32, x.shape, 1)
    shift = 1
    while shift < n:
        ahead = pltpu.roll(x, n - shift, 1)
        x = x + jnp.where(lane < n - shift, ahead, 0.0)
        shift *= 2
    return x


def _decode_group(slot, q_ref, k_buf, v_buf, acc_ref, after_ref, hrow, bias):
    _, n_pages, width, page = k_buf.shape
    heads = width // HEAD_DIM
    scores = [jnp.zeros((heads, page), F32)] * n_pages
    for h in range(heads):
        rows = slice(h * HEAD_DIM, (h + 1) * HEAD_DIM)
        qh = q_ref[0, rows, :]
        for j in range(n_pages):
            dot = jnp.sum(k_buf[slot, j, rows, :] * qh, axis=0, keepdims=True)
            scores[j] = jnp.where(hrow == h, dot, scores[j])
    log_beta, log_keep = [], []
    for s in scores:
        lb, lk = _stick_logs(s + bias)
        log_beta.append(lb)
        log_keep.append(lk)
    keep = jnp.concatenate(log_keep, axis=0)
    from_here = _suffix_sums(keep)
    after = after_ref[...]
    weights = [None] * n_pages
    for j in reversed(range(n_pages)):
        rows = slice(j * heads, (j + 1) * heads)
        weights[j] = jnp.exp(log_beta[j] + (from_here[rows] - keep[rows]) + after)
        after = after + from_here[rows, :1]
    after_ref[...] = after
    for h in range(heads):
        rows = slice(h * HEAD_DIM, (h + 1) * HEAD_DIM)
        part = acc_ref[rows, :]
        for j in range(n_pages):
            part = part + v_buf[slot, j, rows, :] * weights[j][h:h + 1, :]
        acc_ref[rows, :] = part


def _shift_rows(x, s, fill):
    n, c = x.shape
    if s % SUBLANES == 0:
        return jnp.concatenate([jnp.full((s, c), fill, x.dtype), x[:n - s]], axis=0)
    rows = lax.broadcasted_iota(jnp.int32, x.shape, 0)
    return jnp.where(rows < s, fill, pltpu.roll(x, s, 0))


def _lru_gates(xc, wa_ref, ba_ref, wx_ref, bx_ref, neg_log_base):
    xb = xc.astype(BF16)
    r = jax.nn.sigmoid(jnp.dot(xb, wa_ref[...], preferred_element_type=F32) + ba_ref[...])
    gate = jax.nn.sigmoid(jnp.dot(xb, wx_ref[...], preferred_element_type=F32) + bx_ref[...])
    a = jnp.exp(-LRU_C * r * neg_log_base)
    return a, jnp.sqrt(1.0 - a * a), gate


def _pool_project(d_groups, pw_ref, ps_ref):
    y = [jnp.dot(d.astype(BF16), pw_ref[g], preferred_element_type=F32) for g, d in enumerate(d_groups)]
    return jnp.concatenate(y, axis=1) * ps_ref[...]


def _prompt_mixer_kernel(xr_ref, xp_ref, cw_ref, cb_ref, wa_ref, ba_ref, wx_ref, bx_ref, lam_ref,
                         pw_ref, ps_ref, ob_ref, oc_ref, hl_ref, *, chunk):
    t = xr_ref.shape[1]
    width = xr_ref.shape[2]
    group = width // len(POOL_WINDOWS)
    halo = 2 * SUBLANES
    neg_log_base = _softplus(-lam_ref[...])
    rows = lax.broadcasted_iota(jnp.int32, (chunk, 1), 0)

    def with_halo(ref, base, c):
        before = pl.multiple_of(jnp.maximum(base - halo, 0), SUBLANES)
        prev = jnp.where(c > 0, ref[0, pl.ds(before, halo), :], 0.0)
        return jnp.concatenate([prev, ref[0, pl.ds(base, chunk), :]], axis=0)

    def body(c, h_in):
        base = pl.multiple_of(c * chunk, chunk)
        pos = rows + base
        xs = with_halo(xr_ref, base, c)
        xc = cb_ref[...]
        for j in range(CONV_W):
            back = CONV_W - 1 - j
            tap = xs[halo:] if back == 0 else pltpu.roll(xs, back, 0)[halo:]
            xc = xc + tap * cw_ref[j:j + 1, :]
        a, mult, gate = _lru_gates(xc, wa_ref, ba_ref, wx_ref, bx_ref, neg_log_base)
        reset = pos == 0
        a = jnp.where(reset, 0.0, a)
        mult = jnp.where(reset, 1.0, mult)
        u = mult * gate * xc
        s = 1
        while s < chunk:
            a_prev = _shift_rows(a, s, 1.0)
            u_prev = _shift_rows(u, s, 0.0)
            u = u + a * u_prev
            a = a * a_prev
            s *= 2
        h = u + a * h_in
        ob_ref[0, pl.ds(base, chunk), :] = h.astype(ob_ref.dtype)
        ps = with_halo(xp_ref, base, c)
        diffs = []
        for g, win in enumerate(POOL_WINDOWS):
            x = ps[:, g * group:(g + 1) * group]
            total = x
            s = 1
            while s < win:
                total = total + _shift_rows(total, s, 0.0)
                s *= 2
            cnt = jnp.minimum(pos + 1, win).astype(F32)
            diffs.append(total[halo:] / cnt - x[halo:])
        oc_ref[0, pl.ds(base, chunk), :] = _pool_project(diffs, pw_ref, ps_ref).astype(oc_ref.dtype)
        return h[chunk - 1:chunk, :]

    h_last = lax.fori_loop(0, t // chunk, body, jnp.zeros((1, width), F32))
    hl_ref[0] = h_last


def _mixer_weights(lw):
    return (lw["conv_w"], lw["conv_b"], lw["gate_a_w"], lw["gate_a_b"], lw["gate_x_w"], lw["gate_x_b"],
            lw["lru_lambda"], lw["pool_w"], lw["pool_scale"])


def _prompt_mixers(xr, xp, lw, chunk):
    bsz, t, width = xr.shape
    seq = pl.BlockSpec((1, t, width), lambda b: (b, 0, 0))
    weights = _mixer_weights(lw)
    return pl.pallas_call(
        functools.partial(_prompt_mixer_kernel, chunk=chunk),
        grid=(bsz,),
        in_specs=[seq, seq] + [_resident(w.shape) for w in weights],
        out_specs=[seq, seq, pl.BlockSpec((1, 1, width), lambda b: (b, 0, 0))],
        out_shape=[jax.ShapeDtypeStruct((bsz, t, width), BF16), jax.ShapeDtypeStruct((bsz, t, width), BF16),
                   jax.ShapeDtypeStruct((bsz, 1, width), F32)],
        compiler_params=_params("parallel"),
        name="prompt_mixers",
    )(xr, xp, *weights)


def _sample_mixer_kernel(xr_ref, xp_ref, ch_ref, ph_ref, h0_ref, cw_ref, cb_ref, wa_ref, ba_ref, wx_ref, bx_ref,
                         lam_ref, pw_ref, ps_ref, ob_ref, oc_ref, h_ref):
    width = xr_ref.shape[1]
    group = width // len(POOL_WINDOWS)
    xc = cb_ref[...]
    for j in range(CONV_W - 1):
        xc = xc + ch_ref[j] * cw_ref[j:j + 1, :]
    xc = xc + xr_ref[...] * cw_ref[CONV_W - 1:CONV_W, :]
    a, mult, gate = _lru_gates(xc, wa_ref, ba_ref, wx_ref, bx_ref, _softplus(-lam_ref[...]))
    h = mult * gate * xc + a * h0_ref[...]
    h_ref[...] = h
    ob_ref[...] = h.astype(ob_ref.dtype)
    n_hist = ph_ref.shape[0]
    diffs = []
    for g, win in enumerate(POOL_WINDOWS):
        cols = slice(g * group, (g + 1) * group)
        x = xp_ref[:, cols]
        total = x
        for back in range(1, win):
            total = total + ph_ref[n_hist - back][:, cols]
        diffs.append(total / float(win) - x)
    oc_ref[...] = _pool_project(diffs, pw_ref, ps_ref).astype(oc_ref.dtype)


def _sample_mixers(xr, xp, conv_hist, pool_hist, h0, lw):
    bsz, width = xr.shape
    return pl.pallas_call(
        _sample_mixer_kernel,
        out_shape=[jax.ShapeDtypeStruct((bsz, width), BF16), jax.ShapeDtypeStruct((bsz, width), BF16),
                   jax.ShapeDtypeStruct((bsz, width), F32)],
        compiler_params=pltpu.CompilerParams(vmem_limit_bytes=VMEM_LIMIT_BYTES),
        name="sample_mixers",
    )(xr, xp, conv_hist, pool_hist, h0, *_mixer_weights(lw))


def _merge_mlp_kernel(x_ref, oa_ref, ob_ref, oc_ref, g_pre_ref, wg_ref, wb_ref, wo_ref, g_post_ref,
                      g_mlp_pre_ref, wu_ref, wd_ref, g_mlp_post_ref, y_ref):
    d = x_ref.shape[1]
    x = x_ref[...]
    u = _rms(x, g_pre_ref[...]).astype(BF16)
    mixed = None
    for n, branch in enumerate((oa_ref, ob_ref, oc_ref)):
        gate = jax.nn.sigmoid(jnp.dot(u, wg_ref[:, n * d:(n + 1) * d], preferred_element_type=F32))
        part = gate * jnp.dot(branch[...], wb_ref[n], preferred_element_type=F32)
        mixed = part if mixed is None else mixed + part
    mix_out = jnp.dot(mixed.astype(BF16), wo_ref[...], preferred_element_type=F32)
    x = x + _rms(mix_out, g_post_ref[...])
    u2 = _rms(x, g_mlp_pre_ref[...]).astype(BF16)
    hid = jnp.square(jnp.maximum(jnp.dot(u2, wu_ref[...], preferred_element_type=F32), 0.0))
    ffn = jnp.dot(hid.astype(BF16), wd_ref[...], preferred_element_type=F32)
    y_ref[...] = x + _rms(ffn, g_mlp_post_ref[...])


def _merge_mlp(x, oa, ob, oc, lw, tm):
    rows, d = x.shape
    width = oa.shape[1]
    weights = (lw["norm_mix_pre"], lw["w_gate"], lw["w_branch"], lw["w_out"], lw["norm_mix_post"],
               lw["norm_mlp_pre"], lw["w_up"], lw["w_down"], lw["norm_mlp_post"])
    row = pl.BlockSpec((tm, d), lambda i: (i, 0))
    branch = pl.BlockSpec((tm, width), lambda i: (i, 0))
    return pl.pallas_call(
        _merge_mlp_kernel,
        grid=(rows // tm,),
        in_specs=[row, branch, branch, branch] + [_resident(w.shape) for w in weights],
        out_specs=row,
        out_shape=jax.ShapeDtypeStruct((rows, d), F32),
        compiler_params=_params("parallel"),
        name="merge_mlp",
    )(x, oa, ob, oc, *weights)


def _merge_mlp_decode_kernel(pt_ref, bias_ref, q_ref, pool_k, pool_v, x_ref, oa_ref, ob_ref, oc_ref, g_pre_ref,
                             wg_ref, wb_ref, wo_ref, g_post_ref, g_mlp_pre_ref, wu_ref, wd_ref, g_mlp_post_ref,
                             y_ref, o_ref, k_buf, v_buf, sems, acc_ref, after_ref, *, n_slots, layer_base, n_regions):
    step = pl.program_id(0)
    d = x_ref.shape[1]
    _, n_pages, width, page = k_buf.shape
    heads = width // HEAD_DIM
    groups_per_seq = pt_ref.shape[1] // n_pages
    total = pl.num_programs(0) * n_regions
    n_branch = wb_ref.shape[0]
    n_ff = n_regions - n_branch - 1
    ff = wu_ref.shape[1] // n_ff
    hrow, bias = _decode_bias(bias_ref, heads, page)

    def group_copies(n, slot):
        b = n // groups_per_seq
        grp = n % groups_per_seq
        copies = []
        for j in range(n_pages):
            src = layer_base + pt_ref[b, (groups_per_seq - 1 - grp) * n_pages + j]
            copies.append(pltpu.make_async_copy(pool_k.at[src], k_buf.at[slot, j], sems.at[0, slot]))
            copies.append(pltpu.make_async_copy(pool_v.at[src], v_buf.at[slot, j], sems.at[1, slot]))
        return copies

    def start_group(n):
        for copy in group_copies(n, n % n_slots):
            copy.start()

    def begin_piece(r):
        n = step * n_regions + r
        ahead = n + (n_slots - 1)
        if r + (n_slots - 1) < n_regions:
            start_group(ahead)
        else:
            @pl.when(ahead < total)
            def _():
                start_group(ahead)

        slot = n % n_slots
        for copy in group_copies(n, slot):
            copy.wait()

        if r == 0:
            @pl.when(n % groups_per_seq == 0)
            def _():
                acc_ref[...] = jnp.zeros_like(acc_ref)
                after_ref[...] = jnp.zeros_like(after_ref)

        return slot

    def end_piece(r, slot):
        _decode_group(slot, q_ref, k_buf, v_buf, acc_ref, after_ref, hrow, bias)

        if r == n_regions - 1:
            @pl.when((step * n_regions + r) % groups_per_seq == groups_per_seq - 1)
            def _():
                o_ref[0] = jnp.sum(acc_ref[...], axis=1, keepdims=True)

    @pl.when(step == 0)
    def _():
        for n in range(n_slots - 1):
            start_group(n)

    x = x_ref[...]
    u = _rms(x, g_pre_ref[...]).astype(BF16)
    mixed = None
    for n, branch in enumerate((oa_ref, ob_ref, oc_ref)):
        slot = begin_piece(n)
        gate = jax.nn.sigmoid(jnp.dot(u, wg_ref[:, n * d:(n + 1) * d], preferred_element_type=F32))
        part = gate * jnp.dot(branch[...], wb_ref[n], preferred_element_type=F32)
        mixed = part if mixed is None else mixed + part
        end_piece(n, slot)
    slot = begin_piece(n_branch)
    mix_out = jnp.dot(mixed.astype(BF16), wo_ref[...], preferred_element_type=F32)
    x = x + _rms(mix_out, g_post_ref[...])
    u2 = _rms(x, g_mlp_pre_ref[...]).astype(BF16)
    up = lambda c: jnp.square(jnp.maximum(
        jnp.dot(u2, wu_ref[:, c * ff:(c + 1) * ff], preferred_element_type=F32), 0.0)).astype(BF16)
    hid = up(0)
    end_piece(n_branch, slot)
    ffn = None
    for c in range(n_ff):
        slot = begin_piece(n_branch + 1 + c)
        down = jnp.dot(hid, wd_ref[c * ff:(c + 1) * ff, :], preferred_element_type=F32)
        ffn = down if ffn is None else ffn + down
        if c + 1 < n_ff:
            hid = up(c + 1)
        end_piece(n_branch + 1 + c, slot)
    y_ref[...] = x + _rms(ffn, g_mlp_post_ref[...])


def _merge_mlp_decode(x, oa, ob, oc, lw, tm, q_lanes, pool_k, pool_v, page_table, layer_base, n_pages, n_slots):
    rows, d = x.shape
    bsz, width, page = q_lanes.shape
    heads = width // HEAD_DIM
    weights = (lw["norm_mix_pre"], lw["w_gate"], lw["w_branch"], lw["w_out"], lw["norm_mix_post"],
               lw["norm_mlp_pre"], lw["w_up"], lw["w_down"], lw["norm_mlp_post"])
    n_steps = rows // tm
    groups_per_seq = page_table.shape[1] // n_pages
    n_regions = bsz * groups_per_seq // n_steps
    assert n_regions * n_steps == bsz * groups_per_seq and groups_per_seq % n_regions == 0
    assert n_regions > lw["w_branch"].shape[0] + 1 and n_slots - 1 <= n_regions
    steps_per_seq = groups_per_seq // n_regions
    row = pl.BlockSpec((tm, d), lambda i, pt: (i, 0))
    branch = pl.BlockSpec((tm, oa.shape[1]), lambda i, pt: (i, 0))
    seq = lambda last: pl.BlockSpec((1, width, last), lambda i, pt: (i // steps_per_seq, 0, 0))
    resident = lambda w: pl.BlockSpec(w.shape, lambda i, pt: (0,) * w.ndim, pipeline_mode=pl.Buffered(1))
    grid_spec = pltpu.PrefetchScalarGridSpec(
        num_scalar_prefetch=1,
        grid=(n_steps,),
        in_specs=[pl.BlockSpec(memory_space=pltpu.SMEM), seq(page),
                  pl.BlockSpec(memory_space=pl.ANY), pl.BlockSpec(memory_space=pl.ANY),
                  row, branch, branch, branch] + [resident(w) for w in weights],
        out_specs=[row, seq(1)],
        scratch_shapes=[pltpu.VMEM((n_slots, n_pages, width, page), F32),
                        pltpu.VMEM((n_slots, n_pages, width, page), F32),
                        pltpu.SemaphoreType.DMA((2, n_slots)),
                        pltpu.VMEM((width, page), F32), pltpu.VMEM((heads, page), F32)],
    )
    return pl.pallas_call(
        functools.partial(_merge_mlp_decode_kernel, n_slots=n_slots, layer_base=layer_base, n_regions=n_regions),
        grid_spec=grid_spec,
        out_shape=[jax.ShapeDtypeStruct((rows, d), F32), jax.ShapeDtypeStruct((bsz, width, 1), F32)],
        compiler_params=_params("arbitrary"),
        name="merge_mlp_decode",
    )(page_table, lw["sb_bias"], q_lanes, pool_k, pool_v, x, oa, ob, oc, *weights)


def _block_diag(w):
    n, c, _ = w.shape
    eye = jnp.eye(n, dtype=w.dtype)
    return jnp.einsum("nij,nm->nimj", w, eye).reshape(n * c, n * c)


def _layer_weights(l, width, p):
    row = lambda a: a[l].reshape(1, -1)
    w_in = p["w_in"][l].astype(BF16)
    return {
        "norm_mix_pre": row(p["norm_mix_pre"]), "norm_mix_post": row(p["norm_mix_post"]),
        "norm_mlp_pre": row(p["norm_mlp_pre"]), "norm_mlp_post": row(p["norm_mlp_post"]),
        "w_seq": w_in[:, :5 * width], "w_q": w_in[:, :width], "w_kv_t": w_in[:, width:3 * width].T,
        "w_rp": w_in[:, 3 * width:5 * width], "w_gate": w_in[:, 5 * width:],
        "sb_bias": p["sb_bias"][l],
        "conv_w": p["conv_w"][l], "conv_b": row(p["conv_b"]),
        "gate_a_w": _block_diag(p["gate_a_w"][l]).astype(BF16), "gate_a_b": row(p["gate_a_b"]),
        "gate_x_w": _block_diag(p["gate_x_w"][l]).astype(BF16), "gate_x_b": row(p["gate_x_b"]),
        "lru_lambda": row(p["lru_lambda"]),
        "pool_w": p["pool_w"][l].astype(BF16), "pool_scale": row(p["pool_scale"]),
        "w_branch": p["w_branch"][l].astype(BF16), "w_out": p["w_out"][l].astype(BF16),
        "w_up": p["w_up"][l].astype(BF16), "w_down": p["w_down"][l].astype(BF16),
    }


def kernel(x_prompt, x_sample, cache_k, cache_v, page_table, state_h, state_conv, state_pool, norm_mix_pre, norm_mix_post, norm_mlp_pre, norm_mlp_post, w_in, sb_bias, conv_w, conv_b, gate_a_w, gate_a_b, gate_x_w, gate_x_b, lru_lambda, pool_w, pool_scale, w_branch, w_out, w_up, w_down):
    params = dict(norm_mix_pre=norm_mix_pre, norm_mix_post=norm_mix_post, norm_mlp_pre=norm_mlp_pre,
                  norm_mlp_post=norm_mlp_post, w_in=w_in, sb_bias=sb_bias, conv_w=conv_w, conv_b=conv_b,
                  gate_a_w=gate_a_w, gate_a_b=gate_a_b, gate_x_w=gate_x_w, gate_x_b=gate_x_b,
                  lru_lambda=lru_lambda, pool_w=pool_w, pool_scale=pool_scale, w_branch=w_branch,
                  w_out=w_out, w_up=w_up, w_down=w_down)
    b_p, seq, d = x_prompt.shape
    b_s, dec_seq, _ = x_sample.shape
    assert dec_seq == 1, "the sample group decodes one token per sequence"
    depth, n_phys, page, heads, head_dim = cache_k.shape
    assert head_dim == HEAD_DIM
    width = heads * head_dim
    n_conv, n_pool = CONV_W - 1, max(POOL_WINDOWS) - 1
    assert page_table.shape[1] * page > n_pool, "sample positions must have complete conv / pooling windows"
    pool_k = jnp.transpose(cache_k, (0, 1, 3, 4, 2)).reshape(depth * n_phys, width, page)
    pool_v = jnp.transpose(cache_v, (0, 1, 3, 4, 2)).reshape(depth * n_phys, width, page)

    def rows_to_heads(stack_t):
        return jnp.transpose(stack_t.reshape(depth, b_p, heads, head_dim, seq), (0, 1, 4, 2, 3))

    yp = x_prompt
    ys = x_sample.reshape(b_s, d)
    outs = {name: [] for name in ("ks", "vs", "hp", "hs", "cp", "cs", "pp", "ps")}
    kv_stacks = []
    for l in range(depth):
        lw = _layer_weights(l, width, params)
        qb, *kv_stacks, kvb, xr, xp = _prompt_in_proj(yp, lw["norm_mix_pre"], lw["w_q"], lw["w_kv_t"], lw["w_rp"],
                                                      ATT_TILE, l, depth, kv_stacks)
        oa = _prompt_attention(qb, kvb, lw["sb_bias"])
        ob, oc, h_last = _prompt_mixers(xr, xp, lw, SEQ_CHUNK)
        q, k, v, xr_s, xp_s = _sample_in_proj(ys, lw["norm_mix_pre"], lw["w_seq"])
        q_lanes = jnp.broadcast_to(q[:, :, None], (b_s, width, page))
        flat = lambda a: a.reshape(b_p * seq, -1)
        yp, oa = _merge_mlp_decode(flat(yp), flat(oa), flat(ob), flat(oc), lw, ROW_TILE, q_lanes, pool_k, pool_v,
                                   page_table, l * n_phys, PAGES_PER_STEP, PAGE_RING_SLOTS)
        yp = yp.reshape(b_p, seq, d)
        outs["hp"].append(h_last.reshape(b_p, width))
        outs["cp"].append(xr[:, seq - n_conv:])
        outs["pp"].append(xp[:, seq - n_pool:])
        xr, xp = xr_s, xp_s
        ob, oc, h_new = _sample_mixers(xr, xp, jnp.swapaxes(state_conv[l], 0, 1), jnp.swapaxes(state_pool[l], 0, 1),
                                       state_h[l], lw)
        ys = _merge_mlp(ys, oa.reshape(b_s, width).astype(BF16), ob, oc, lw, b_s)
        outs["ks"].append(k.reshape(b_s, 1, heads, head_dim))
        outs["vs"].append(v.reshape(b_s, 1, heads, head_dim))
        outs["hs"].append(h_new)
        outs["cs"].append(jnp.concatenate([state_conv[l][:, 1:], xr[:, None, :]], axis=1))
        outs["ps"].append(jnp.concatenate([state_pool[l][:, 1:], xp[:, None, :]], axis=1))

    stack = lambda name: jnp.stack(outs[name])
    return (yp, ys.reshape(b_s, 1, d), rows_to_heads(kv_stacks[0]), rows_to_heads(kv_stacks[1]), stack("ks"), stack("vs"),
            stack("hp"), stack("hs"), stack("cp"), stack("cs"), stack("pp"), stack("ps"))
```
